```python
import jax, jax.numpy as jnp
from jax import lax
import numpy as np

D_MODEL = 1024
BATCH = 4
SEQ = 4096
DEPTH = 1

HEAD_DIM = 64
NSA_HEADS = 8
NSA_KV_HEADS = 2
SWA_HEADS = 8
SWA_KV_HEADS = 2
CMP_BLOCK = 32
CMP_STRIDE = 16
CMP_HIDDEN = 256
SLC_BLOCK = 64
N_SELECT = 16
NSA_WINDOW = 512
SWA_WINDOW = 128
Q_BLOCK = 128
RMS_EPS = 1e-6
NEG = -1e30
BIG = 1e30
NSA_WIDTH = NSA_HEADS * HEAD_DIM
SWA_WIDTH = SWA_HEADS * HEAD_DIM
KV_NSA = NSA_KV_HEADS * HEAD_DIM
KV_SWA = SWA_KV_HEADS * HEAD_DIM
SPLIT_SIZES = (NSA_WIDTH, 2 * KV_NSA, 2 * KV_NSA, 2 * KV_NSA, 3 * NSA_HEADS, NSA_WIDTH,
               SWA_WIDTH, 2 * KV_SWA, SWA_WIDTH, 2 * D_MODEL)
D_IN = sum(SPLIT_SIZES)

kernel_name = 'hybrid_nsa_swa_sink_gated_block'


def split_points():
    pts, acc = [], 0
    for s in SPLIT_SIZES[:-1]:
        acc += s
        pts.append(acc)
    return pts


def alibi_slopes(n, n_kv):
    s = 2.0 ** (-8.0 * (np.arange(n) + 1) / n)
    return jnp.asarray(s, dtype=jnp.float32).reshape(n_kv, n // n_kv)


def rms_norm(x, g):
    xf = x.astype(jnp.float32)
    y = xf * lax.rsqrt(jnp.mean(xf * xf, axis=-1, keepdims=True) + RMS_EPS)
    return (y * g.astype(jnp.float32)).astype(x.dtype)


def q_heads(t, n_kv, r):
    B, S, _ = t.shape
    return t.reshape(B, S, n_kv, r, HEAD_DIM).transpose(0, 2, 3, 1, 4)


def kv_heads(t, n_kv):
    B, S, _ = t.shape
    return t.reshape(B, S, n_kv, HEAD_DIM).transpose(0, 2, 1, 3)


def merge_heads(o):
    B, G, r, S, dh = o.shape
    return o.transpose(0, 3, 1, 2, 4).reshape(B, S, G * r * dh)


def banded_attention(q, k, v, slopes, window, sinks=None):
    B, G, r, S, dh = q.shape
    nprev = window // Q_BLOCK
    nqb = S // Q_BLOCK
    lk = (nprev + 1) * Q_BLOCK
    pad = nprev * Q_BLOCK
    kp = jnp.pad(k, ((0, 0), (0, 0), (pad, 0), (0, 0)))
    vp = jnp.pad(v, ((0, 0), (0, 0), (pad, 0), (0, 0)))
    idx = jnp.arange(nqb)[:, None] * Q_BLOCK + jnp.arange(lk)[None, :]
    kb = kp[:, :, idx].astype(jnp.float32)
    vb = vp[:, :, idx].astype(jnp.float32)
    qb = q.reshape(B, G, r, nqb, Q_BLOCK, dh).astype(jnp.float32)
    s = jnp.einsum('bgrnqd,bgnkd->bgrnqk', qb, kb) * (dh ** -0.5)
    qpos = jnp.arange(nqb)[:, None] * Q_BLOCK + jnp.arange(Q_BLOCK)[None, :]
    kpos = idx - pad
    dist = qpos[:, :, None] - kpos[:, None, :]
    valid = (dist >= 0) & (dist < window) & (kpos[:, None, :] >= 0)
    s = s - slopes[None, :, :, None, None, None] * dist.astype(jnp.float32)
    s = jnp.where(valid, s, NEG)
    if sinks is not None:
        sink = jnp.broadcast_to(sinks.astype(jnp.float32).reshape(1, G, r, 1, 1, 1), s.shape[:-1] + (1,))
        p = jax.nn.softmax(jnp.concatenate([s, sink], axis=-1), axis=-1)[..., :-1]
    else:
        p = jax.nn.softmax(s, axis=-1)
    o = jnp.einsum('bgrnqk,bgnkd->bgrnqd', p, vb)
    return o.reshape(B, G, r, S, dh).astype(q.dtype)


def compress_blocks(t, pe, w1, w2):
    B, G, S, dh = t.shape
    nc = (S - CMP_BLOCK) // CMP_STRIDE + 1
    idx = jnp.arange(nc)[:, None] * CMP_STRIDE + jnp.arange(CMP_BLOCK)[None, :]
    blk = t[:, :, idx] + pe
    flat = blk.reshape(B, G, nc, CMP_BLOCK * dh)
    return jax.nn.silu(flat @ w1) @ w2


def compressed_attention(q, kc, vc, slopes):
    B, G, r, S, dh = q.shape
    nc = kc.shape[2]
    s = jnp.einsum('bgrqd,bgcd->bgrqc', q.astype(jnp.float32), kc.astype(jnp.float32)) * (dh ** -0.5)
    end = jnp.arange(nc) * CMP_STRIDE + CMP_BLOCK - 1
    dist = jnp.arange(S)[:, None] - end[None, :]
    valid = dist >= 0
    s = s - slopes[None, :, :, None, None] * dist.astype(jnp.float32)
    s = jnp.where(valid, s, NEG)
    p = jax.nn.softmax(s, axis=-1) * jnp.any(valid, axis=-1)[:, None].astype(jnp.float32)
    o = jnp.einsum('bgrqc,bgcd->bgrqd', p, vc.astype(jnp.float32))
    return o.astype(q.dtype), p


def select_blocks(p_cmp):
    p = p_cmp.sum(axis=2)
    S, nc = p.shape[2], p.shape[3]
    nsb = S // SLC_BLOCK
    ratio = SLC_BLOCK // CMP_STRIDE
    span = CMP_BLOCK // CMP_STRIDE
    offs = (jnp.arange(ratio)[:, None] + jnp.arange(span)[None, :]).reshape(-1)
    cidx = ratio * jnp.arange(nsb)[:, None] - offs[None, :]
    ok = (cidx >= 0) & (cidx < nc)
    imp = jnp.where(ok, p[..., jnp.clip(cidx, 0, nc - 1)], 0.0).sum(axis=-1)
    cur = jnp.arange(S) // SLC_BLOCK
    j = jnp.arange(nsb)
    causal = j[None, :] <= cur[:, None]
    forced = (j[None, :] == 0) | (j[None, :] == cur[:, None]) | (j[None, :] == cur[:, None] - 1)
    score = jnp.where(causal & forced, BIG, jnp.where(causal, imp, NEG))
    _, idx = lax.top_k(score, min(N_SELECT, nsb))
    return idx


def selected_attention(q, k, v, blk_idx, slopes):
    B, G, r, S, dh = q.shape
    nsb = S // SLC_BLOCK
    n_sel = blk_idx.shape[-1]
    nqc = S // Q_BLOCK
    kb = k.reshape(B, G, nsb, SLC_BLOCK, dh)
    vb = v.reshape(B, G, nsb, SLC_BLOCK, dh)
    qc = q.reshape(B, G, r, nqc, Q_BLOCK, dh).transpose(3, 0, 1, 2, 4, 5)
    ic = blk_idx.reshape(B, G, nqc, Q_BLOCK, n_sel).transpose(2, 0, 1, 3, 4)
    qpos = jnp.arange(S).reshape(nqc, Q_BLOCK)
    gather = jax.vmap(jax.vmap(lambda blocks, ix: blocks[ix]))

    def one_block(args):
        q_i, ix, qp = args
        k_sel = gather(kb, ix).astype(jnp.float32)
        v_sel = gather(vb, ix).astype(jnp.float32)
        kpos = ix[..., None] * SLC_BLOCK + jnp.arange(SLC_BLOCK)
        dist = (qp[None, None, :, None, None] - kpos)[:, :, None]
        s = jnp.einsum('bgrqd,bgqnkd->bgrqnk', q_i.astype(jnp.float32), k_sel) * (dh ** -0.5)
        s = s - slopes[None, :, :, None, None, None] * dist.astype(jnp.float32)
        s = jnp.where(dist >= 0, s, NEG).reshape(B, G, r, Q_BLOCK, n_sel * SLC_BLOCK)
        p = jax.nn.softmax(s, axis=-1)
        return jnp.einsum('bgrqk,bgqkd->bgrqd', p, v_sel.reshape(B, G, Q_BLOCK, n_sel * SLC_BLOCK, dh))

    o = lax.map(one_block, (qc, ic, qpos))
    return o.transpose(1, 2, 3, 0, 4, 5).reshape(B, G, r, S, dh).astype(q.dtype)


def setup_inputs(seed: int = 0) -> dict:
    key = jax.random.key(seed)
    ks = jax.random.split(key, 20)
    D, L, dh = D_MODEL, DEPTH, HEAD_DIM
    nrm = lambda k, shape, fan: jax.random.normal(k, shape, jnp.float32) * (fan ** -0.5)
    return {
        'x': jax.random.normal(ks[0], (BATCH, SEQ, D), jnp.float32),
        'c': jax.random.normal(ks[1], (BATCH, D), jnp.float32),
        'w_ada': nrm(ks[2], (L, D, 3 * D), D) * 0.5,
        'b_ada': 0.01 * jax.random.normal(ks[3], (L, 3 * D), jnp.float32),
        'g_pre': 1.0 + 0.05 * jax.random.normal(ks[4], (L, D), jnp.float32),
        'g_post': 1.0 + 0.05 * jax.random.normal(ks[5], (L, D), jnp.float32),
        'w_in': nrm(ks[6], (L, D, D_IN), D),
        'pe_cmp_k': 0.02 * jax.random.normal(ks[7], (L, CMP_BLOCK, dh), jnp.float32),
        'pe_cmp_v': 0.02 * jax.random.normal(ks[8], (L, CMP_BLOCK, dh), jnp.float32),
        'w_cmp_k1': nrm(ks[9], (L, CMP_BLOCK * dh, CMP_HIDDEN), CMP_BLOCK * dh),
        'w_cmp_k2': nrm(ks[10], (L, CMP_HIDDEN, dh), CMP_HIDDEN),
        'w_cmp_v1': nrm(ks[11], (L, CMP_BLOCK * dh, CMP_HIDDEN), CMP_BLOCK * dh),
        'w_cmp_v2': nrm(ks[12], (L, CMP_HIDDEN, dh), CMP_HIDDEN),
        'w_o_nsa': nrm(ks[13], (L, NSA_WIDTH, D), NSA_WIDTH),
        'w_o_swa': nrm(ks[14], (L, SWA_WIDTH, D), SWA_WIDTH),
        'w_out': nrm(ks[15], (L, D, D), D),
        'sinks': jax.random.normal(ks[16], (L, SWA_HEADS), jnp.float32),
    }


def reference(x, c, w_ada, b_ada, g_pre, g_post, w_in, pe_cmp_k, pe_cmp_v, w_cmp_k1, w_cmp_k2,
              w_cmp_v1, w_cmp_v2, w_o_nsa, w_o_swa, w_out, sinks):
    B, S, D = x.shape
    ra = NSA_HEADS // NSA_KV_HEADS
    rb = SWA_HEADS // SWA_KV_HEADS
    slopes_a = alibi_slopes(NSA_HEADS, NSA_KV_HEADS)
    slopes_b = alibi_slopes(SWA_HEADS, SWA_KV_HEADS)
    pts = split_points()
    for l in range(DEPTH):
        mod = c @ w_ada[l] + b_ada[l]
        shift, scale, gate = jnp.split(mod, 3, axis=-1)
        h = rms_norm(x, g_pre[l]) * (1.0 + scale[:, None, :]) + shift[:, None, :]
        proj = h @ w_in[l]
        q_a, kv_c, kv_s, kv_w, g_nsa, z_a, q_b, kv_b, z_b, merge = jnp.split(proj, pts, axis=-1)

        qa = q_heads(q_a, NSA_KV_HEADS, ra)
        kc_raw, vc_raw = jnp.split(kv_c, 2, axis=-1)
        ks_, vs_ = jnp.split(kv_s, 2, axis=-1)
        kw_, vw_ = jnp.split(kv_w, 2, axis=-1)
        kc = compress_blocks(kv_heads(kc_raw, NSA_KV_HEADS), pe_cmp_k[l], w_cmp_k1[l], w_cmp_k2[l])
        vc = compress_blocks(kv_heads(vc_raw, NSA_KV_HEADS), pe_cmp_v[l], w_cmp_v1[l], w_cmp_v2[l])
        o_cmp, p_cmp = compressed_attention(qa, kc, vc, slopes_a)
        blk_idx = select_blocks(p_cmp)
        o_slc = selected_attention(qa, kv_heads(ks_, NSA_KV_HEADS), kv_heads(vs_, NSA_KV_HEADS), blk_idx, slopes_a)
        o_win = banded_attention(qa, kv_heads(kw_, NSA_KV_HEADS), kv_heads(vw_, NSA_KV_HEADS), slopes_a, NSA_WINDOW)
        gts = jax.nn.sigmoid(g_nsa.reshape(B, S, 3, NSA_KV_HEADS, ra)).transpose(2, 0, 3, 4, 1)[..., None]
        o_a = gts[0] * o_cmp + gts[1] * o_slc + gts[2] * o_win
        y_a = (merge_heads(o_a) * jax.nn.silu(z_a)) @ w_o_nsa[l]

        qb = q_heads(q_b, SWA_KV_HEADS, rb)
        kb_, vb_ = jnp.split(kv_b, 2, axis=-1)
        o_b = banded_attention(qb, kv_heads(kb_, SWA_KV_HEADS), kv_heads(vb_, SWA_KV_HEADS), slopes_b,
                               SWA_WINDOW, sinks=sinks[l])
        y_b = (merge_heads(o_b) * jax.nn.silu(z_b)) @ w_o_swa[l]

        m_a, m_b = jnp.split(merge, 2, axis=-1)
        y = (jax.nn.sigmoid(m_a) * y_a + jax.nn.sigmoid(m_b) * y_b) @ w_out[l]
        x = x + gate[:, None, :] * rms_norm(y, g_post[l])
    return x
```

```python
import functools

import jax
import jax.numpy as jnp
from jax import lax
from jax.experimental import pallas as pl
from jax.experimental.pallas import tpu as pltpu

D_MODEL = 1024
HEAD_DIM = 64
N_HEADS = 8
N_KV = 2
GROUP = N_HEADS // N_KV
CMP_BLOCK = 32
CMP_STRIDE = 16
CMP_HIDDEN = 256
SLC_BLOCK = 64
N_SELECT = 16
NSA_WINDOW = 512
SWA_WINDOW = 128
RMS_EPS = 1e-6
NEG = -1e30
BIG = 1e30
M_FLOOR = -1e20

WIDTH = N_HEADS * HEAD_DIM
KVW = N_KV * HEAD_DIM
N_GATES = 3 * N_HEADS
GATE_PAD = 128
GATE_ROWS = 32

VMEM_LIMIT = 56 * 1024 * 1024

TM_PROJ = 512
TQ = 128
TK_SLC = 512
TK_BAND = 128

_NT = (((1,), (1,)), ((), ()))


def _dot(a, b):
    return jnp.dot(a, b, preferred_element_type=jnp.float32)


def _dot_nt(a, b):
    return lax.dot_general(a, b, _NT, preferred_element_type=jnp.float32)


def _sigmoid(v):
    return 1.0 / (1.0 + jnp.exp(-v))


def _adaln_kernel(c_ref, w_ref, b_ref, o_ref):
    o_ref[...] = jnp.dot(c_ref[...], w_ref[...], preferred_element_type=jnp.float32,
                         precision=lax.Precision.HIGHEST) + b_ref[...]


def _adaln_mod(c8, w, b):
    n = w.shape[1]
    bn = 1024
    return pl.pallas_call(
        _adaln_kernel,
        out_shape=jax.ShapeDtypeStruct((c8.shape[0], n), jnp.float32),
        grid=(n // bn,),
        in_specs=[pl.BlockSpec((c8.shape[0], D_MODEL), lambda j: (0, 0)),
                  pl.BlockSpec((D_MODEL, bn), lambda j: (0, j)),
                  pl.BlockSpec((1, bn), lambda j: (0, j))],
        out_specs=pl.BlockSpec((c8.shape[0], bn), lambda j: (0, j)),
        name="adaln_mod",
    )(c8, w, b)


_C_Q = 0
_C_KV = 2 * WIDTH
_C_Z = _C_KV + 8 * KVW
_C_M = _C_Z + 2 * WIDTH
_C_G = _C_M + 2 * D_MODEL
_C_END = _C_G + GATE_PAD


def _in_proj_kernel(x_ref, shift_ref, scale_ref, gpre_ref, w_ref,
                    qa_ref, qb_ref, kc_ref, vc_ref, ks_ref, vst_ref, kw_ref, vwt_ref,
                    kb_ref, vbt_ref, sza_ref, szb_ref, sma_ref, smb_ref, gt_ref):
    xf = x_ref[0]
    ms = jnp.mean(xf * xf, axis=-1, keepdims=True)
    y = xf * lax.rsqrt(ms + RMS_EPS) * gpre_ref[...]
    h = y * (1.0 + scale_ref[0]) + shift_ref[0]
    hb = h.astype(jnp.bfloat16)

    acc = _dot(hb, w_ref[:, _C_Q:_C_KV])
    for g in range(N_KV):
        for r in range(GROUP):
            c = (g * GROUP + r) * HEAD_DIM
            qa_ref[0, g, r] = acc[:, c:c + HEAD_DIM].astype(jnp.bfloat16)
            qb_ref[0, g, r] = acc[:, WIDTH + c:WIDTH + c + HEAD_DIM].astype(jnp.bfloat16)

    acc = _dot(hb, w_ref[:, _C_KV:_C_Z])
    for g in range(N_KV):
        c = g * HEAD_DIM
        kc_ref[0, g] = acc[:, c:c + HEAD_DIM].astype(jnp.bfloat16)
        vc_ref[0, g] = acc[:, KVW + c:KVW + c + HEAD_DIM].astype(jnp.bfloat16)
        ks_ref[0, g] = acc[:, 2 * KVW + c:2 * KVW + c + HEAD_DIM].astype(jnp.bfloat16)
        kw_ref[0, g] = acc[:, 4 * KVW + c:4 * KVW + c + HEAD_DIM].astype(jnp.bfloat16)
        kb_ref[0, g] = acc[:, 6 * KVW + c:6 * KVW + c + HEAD_DIM].astype(jnp.bfloat16)
    vst = acc[:, 3 * KVW:4 * KVW].T
    vwt = acc[:, 5 * KVW:6 * KVW].T
    vbt = acc[:, 7 * KVW:8 * KVW].T
    for g in range(N_KV):
        c = g * HEAD_DIM
        vst_ref[0, g] = vst[c:c + HEAD_DIM].astype(jnp.bfloat16)
        vwt_ref[0, g] = vwt[c:c + HEAD_DIM].astype(jnp.bfloat16)
        vbt_ref[0, g] = vbt[c:c + HEAD_DIM].astype(jnp.bfloat16)

    acc = _dot(hb, w_ref[:, _C_Z:_C_M])
    sz = acc * _sigmoid(acc)
    sza_ref[0] = sz[:, :WIDTH].astype(jnp.bfloat16)
    szb_ref[0] = sz[:, WIDTH:].astype(jnp.bfloat16)

    acc = _dot(hb, w_ref[:, _C_M:_C_M + D_MODEL])
    sma_ref[0] = _sigmoid(acc).astype(jnp.bfloat16)
    acc = _dot(hb, w_ref[:, _C_M + D_MODEL:_C_G])
    smb_ref[0] = _sigmoid(acc).astype(jnp.bfloat16)

    acc = _dot(hb, w_ref[:, _C_G:_C_END])
    gt_ref[0] = _sigmoid(acc).T[:GATE_ROWS]


def _in_proj(x, shift, scale, gpre, w):
    B, S, D = x.shape
    tm = TM_PROJ
    bf = jnp.bfloat16
    q_shape = jax.ShapeDtypeStruct((B, N_KV, GROUP, S, HEAD_DIM), bf)
    k_shape = jax.ShapeDtypeStruct((B, N_KV, S, HEAD_DIM), bf)
    vt_shape = jax.ShapeDtypeStruct((B, N_KV, HEAD_DIM, S), bf)
    q_spec = pl.BlockSpec((1, N_KV, GROUP, tm, HEAD_DIM), lambda b, i: (b, 0, 0, i, 0))
    k_spec = pl.BlockSpec((1, N_KV, tm, HEAD_DIM), lambda b, i: (b, 0, i, 0))
    vt_spec = pl.BlockSpec((1, N_KV, HEAD_DIM, tm), lambda b, i: (b, 0, 0, i))
    row_spec = lambda n: pl.BlockSpec((1, tm, n), lambda b, i: (b, i, 0))
    vec_spec = pl.BlockSpec((1, 1, D), lambda b, i: (b, 0, 0))
    return pl.pallas_call(
        _in_proj_kernel,
        out_shape=(q_shape, q_shape, k_shape, k_shape, k_shape, vt_shape, k_shape, vt_shape,
                   k_shape, vt_shape,
                   jax.ShapeDtypeStruct((B, S, WIDTH), bf), jax.ShapeDtypeStruct((B, S, WIDTH), bf),
                   jax.ShapeDtypeStruct((B, S, D), bf), jax.ShapeDtypeStruct((B, S, D), bf),
                   jax.ShapeDtypeStruct((B, GATE_ROWS, S), jnp.float32)),
        grid=(B, S // tm),
        in_specs=[row_spec(D), vec_spec, vec_spec,
                  pl.BlockSpec((1, D), lambda b, i: (0, 0)),
                  pl.BlockSpec((D, _C_END), lambda b, i: (0, 0), pipeline_mode=pl.Buffered(1))],
        out_specs=(q_spec, q_spec, k_spec, k_spec, k_spec, vt_spec, k_spec, vt_spec,
                   k_spec, vt_spec,
                   row_spec(WIDTH), row_spec(WIDTH), row_spec(D), row_spec(D),
                   pl.BlockSpec((1, GATE_ROWS, tm), lambda b, i: (b, 0, i))),
        compiler_params=pltpu.CompilerParams(
            dimension_semantics=("parallel", "parallel"), vmem_limit_bytes=VMEM_LIMIT),
        name="in_proj",
    )(x, shift, scale, gpre, w)


def _compress_kernel(t_ref, pe_ref, w1_ref, w2_ref, o_ref, ot_ref):
    half = CMP_STRIDE * HEAD_DIM
    t = t_ref[0, 0]
    w1 = w1_ref[0]
    w1b = w1.astype(jnp.bfloat16)
    first = _dot(t, w1b[:half])
    second = _dot(t, w1b[half:])
    n = second.shape[0]
    second = pltpu.roll(second, n - 1, 0)
    pe = jnp.broadcast_to(pe_ref[0], (8, 2 * half))
    bias = jnp.dot(pe, w1, preferred_element_type=jnp.float32,
                   precision=lax.Precision.HIGHEST)[0:1]
    pre = first + second + bias
    hid = pre * _sigmoid(pre)
    out = _dot(hid.astype(jnp.bfloat16), w2_ref[0].astype(jnp.bfloat16))
    o_ref[0, 0] = out.astype(jnp.bfloat16)
    ot_ref[0, 0] = out.T.astype(jnp.bfloat16)


def _compress(t, pe, w1, w2):
    _, BG, n, wdt = t.shape
    return pl.pallas_call(
        _compress_kernel,
        out_shape=(jax.ShapeDtypeStruct((2, BG, n, HEAD_DIM), jnp.bfloat16),
                   jax.ShapeDtypeStruct((2, BG, HEAD_DIM, n), jnp.bfloat16)),
        grid=(2, BG),
        in_specs=[pl.BlockSpec((1, 1, n, wdt), lambda a, i: (a, i, 0, 0)),
                  pl.BlockSpec((1, 1, 2 * wdt), lambda a, i: (a, 0, 0)),
                  pl.BlockSpec((1, 2 * wdt, CMP_HIDDEN), lambda a, i: (a, 0, 0)),
                  pl.BlockSpec((1, CMP_HIDDEN, HEAD_DIM), lambda a, i: (a, 0, 0))],
        out_specs=(pl.BlockSpec((1, 1, n, HEAD_DIM), lambda a, i: (a, i, 0, 0)),
                   pl.BlockSpec((1, 1, HEAD_DIM, n), lambda a, i: (a, i, 0, 0))),
        compiler_params=pltpu.CompilerParams(dimension_semantics=("parallel", "parallel")),
        name="compress",
    )(t, pe, w1, w2)


def _head_slope(g, r):
    return jnp.where(g == 0, 1.0, 2.0 ** -GROUP).astype(jnp.float32) * (2.0 ** -(r + 1))


def _flash_update(carry, s_heads, vt_tile):
    m, l, acc = carry
    s = jnp.concatenate(s_heads, axis=1)
    m_new = jnp.maximum(m, jnp.max(s, axis=0, keepdims=True))
    alpha = jnp.exp(m - m_new)
    p = jnp.exp(s - m_new)
    l = alpha * l + jnp.sum(p, axis=0, keepdims=True)
    acc = alpha * acc + _dot(vt_tile, p.astype(jnp.bfloat16))
    return m_new, l, acc


def _flash_init():
    n = GROUP * TQ
    return (jnp.full((1, n), M_FLOOR, jnp.float32), jnp.zeros((1, n), jnp.float32),
            jnp.zeros((HEAD_DIM, n), jnp.float32))


def _banded_attention(q, k_ref, vt_ref, g, qi, window, sink_of_head):
    tk = TK_BAND
    q0 = qi * TQ
    qpos = q0 + lax.broadcasted_iota(jnp.int32, (tk, TQ), 1)
    krel = lax.broadcasted_iota(jnp.int32, (tk, TQ), 0)

    def body(j, carry):
        k0 = pl.multiple_of(j * tk, tk)
        kt = k_ref[0, 0, pl.ds(k0, tk), :]
        vt = vt_ref[0, 0, :, pl.ds(k0, tk)]
        st = _dot_nt(kt, q)
        dist = qpos - (k0 + krel)
        valid = (dist >= 0) & (dist < window)
        distf = dist.astype(jnp.float32)
        heads = []
        for r in range(GROUP):
            sr = st[:, r * TQ:(r + 1) * TQ] - _head_slope(g, r) * distf
            heads.append(jnp.where(valid, sr, NEG))
        return _flash_update(carry, heads, vt)

    lo = jnp.maximum(qi - window // tk, 0)
    m, l, acc = lax.fori_loop(lo, qi + 1, body, _flash_init())
    if sink_of_head is not None:
        sink = jnp.concatenate(
            [jnp.full((1, TQ), 1.0, jnp.float32) * sink_of_head(r) for r in range(GROUP)], axis=1)
        m_all = jnp.maximum(m, sink)
        alpha = jnp.exp(m - m_all)
        l = alpha * l + jnp.exp(sink - m_all)
        acc = alpha * acc
    return acc / l


def _store_heads(o_t, sz_ref, o_ref):
    stacked = jnp.concatenate([o_t[:, r * TQ:(r + 1) * TQ] for r in range(GROUP)], axis=0)
    o_ref[0] = (stacked.T * sz_ref[0].astype(jnp.float32)).astype(jnp.bfloat16)


def _importance_matrix():
    ratio = SLC_BLOCK // CMP_STRIDE
    span = CMP_BLOCK // CMP_STRIDE
    n_cmp = 256
    j = lax.broadcasted_iota(jnp.int32, (SLC_BLOCK, n_cmp), 0)
    c = lax.broadcasted_iota(jnp.int32, (SLC_BLOCK, n_cmp), 1)
    off = ratio * j - c
    w = jnp.zeros((SLC_BLOCK, n_cmp), jnp.float32)
    for mm in range(ratio):
        for nn in range(span):
            w = w + jnp.where(off == mm + nn, 1.0, 0.0)
    w = jnp.where(c < n_cmp - 1, w, 0.0)
    return w.astype(jnp.bfloat16)


def _nsa_kernel(qa_ref, kc_ref, vct_ref, ks_ref, vst_ref, kw_ref, vwt_ref, gt_ref, sza_ref,
                o_ref, negsel_ref):
    g = pl.program_id(1)
    qi = pl.program_id(2)
    q0 = qi * TQ
    q = qa_ref[0, 0].reshape(GROUP * TQ, HEAD_DIM)

    n_cmp = kc_ref.shape[1]
    st = _dot_nt(kc_ref[0], q)
    qpos_c = q0 + lax.broadcasted_iota(jnp.int32, (n_cmp, TQ), 1)
    end_c = lax.broadcasted_iota(jnp.int32, (n_cmp, TQ), 0) * CMP_STRIDE + (CMP_BLOCK - 1)
    dist = qpos_c - end_c
    valid = dist >= 0
    distf = dist.astype(jnp.float32)
    any_valid = (qpos_c[0:1] >= CMP_BLOCK - 1).astype(jnp.float32)
    p_sum = jnp.zeros((n_cmp, TQ), jnp.float32)
    o_cmp = []
    for r in range(GROUP):
        sr = st[:, r * TQ:(r + 1) * TQ] - _head_slope(g, r) * distf
        sr = jnp.where(valid, sr, NEG)
        mr = jnp.max(sr, axis=0, keepdims=True)
        er = jnp.exp(sr - mr)
        pr = er * (any_valid / jnp.sum(er, axis=0, keepdims=True))
        p_sum = p_sum + pr
        o_cmp.append(_dot(vct_ref[0], pr.astype(jnp.bfloat16)))
    o_cmp = jnp.concatenate(o_cmp, axis=1)

    wimp = _importance_matrix()
    p_hi = p_sum.astype(jnp.bfloat16)
    rem = p_sum - p_hi.astype(jnp.float32)
    p_mid = rem.astype(jnp.bfloat16)
    p_lo = (rem - p_mid.astype(jnp.float32)).astype(jnp.bfloat16)
    imp = _dot(wimp, p_hi) + _dot(wimp, p_mid) + _dot(wimp, p_lo)

    nsb = imp.shape[0]
    jrow = lax.broadcasted_iota(jnp.int32, (nsb, TQ), 0)
    cur = lax.shift_right_logical(q0 + lax.broadcasted_iota(jnp.int32, (nsb, TQ), 1), 6)
    causal = jrow <= cur
    forced = (jrow == 0) | (jrow == cur) | (jrow == cur - 1)
    score = jnp.where(causal, jnp.where(forced, BIG, imp), NEG)
    rank = jnp.zeros((nsb, TQ), jnp.float32)
    for i in range(nsb):
        row = jnp.broadcast_to(score[i:i + 1, :], (nsb, TQ))
        rank = rank + jnp.where(jrow > i, jnp.where(row >= score, 1.0, 0.0),
                                jnp.where(row > score, 1.0, 0.0))
    negsel_ref[...] = jnp.where(rank < float(N_SELECT), 0.0, NEG)

    tk = TK_SLC
    blocks = tk // SLC_BLOCK
    qpos_s = q0 + lax.broadcasted_iota(jnp.int32, (tk, TQ), 1)
    krel_s = lax.broadcasted_iota(jnp.int32, (tk, TQ), 0)

    def slc_body(j, carry):
        k0 = pl.multiple_of(j * tk, tk)
        kt = ks_ref[0, 0, pl.ds(k0, tk), :]
        vt = vst_ref[0, 0, :, pl.ds(k0, tk)]
        s_t = _dot_nt(kt, q)
        rows = negsel_ref[pl.ds(pl.multiple_of(j * blocks, blocks), blocks), :]
        mask = jnp.concatenate(
            [jnp.broadcast_to(rows[c:c + 1, :], (SLC_BLOCK, TQ)) for c in range(blocks)], axis=0)
        dist_s = qpos_s - (k0 + krel_s)
        mask = jnp.where(dist_s >= 0, mask, NEG)
        dist_f = dist_s.astype(jnp.float32)
        heads = [s_t[:, r * TQ:(r + 1) * TQ] - _head_slope(g, r) * dist_f + mask
                 for r in range(GROUP)]
        return _flash_update(carry, heads, vt)

    n_tiles = (q0 + TQ - 1) // tk + 1
    m, l, acc = lax.fori_loop(0, n_tiles, slc_body, _flash_init())
    o_slc = acc / l

    o_win = _banded_attention(q, kw_ref, vwt_ref, g, qi, NSA_WINDOW, None)

    pieces = []
    for r in range(GROUP):
        sl = slice(r * TQ, (r + 1) * TQ)
        gate = [gt_ref[0, pl.ds(br * N_HEADS + g * GROUP + r, 1), :] for br in range(3)]
        pieces.append(gate[0] * o_cmp[:, sl] + gate[1] * o_slc[:, sl] + gate[2] * o_win[:, sl])
    _store_heads(jnp.concatenate(pieces, axis=1), sza_ref, o_ref)


def _nsa(qa, kc, vct, ks, vst, kw, vwt, gt, sza):
    B, _, _, S, _ = qa.shape
    n_cmp = kc.shape[1]
    grid = (B, N_KV, S // TQ)
    full_k = pl.BlockSpec((1, 1, S, HEAD_DIM), lambda b, g, i: (b, g, 0, 0))
    full_vt = pl.BlockSpec((1, 1, HEAD_DIM, S), lambda b, g, i: (b, g, 0, 0))
    return pl.pallas_call(
        _nsa_kernel,
        out_shape=jax.ShapeDtypeStruct((B, S, WIDTH), jnp.bfloat16),
        grid=grid,
        in_specs=[pl.BlockSpec((1, 1, GROUP, TQ, HEAD_DIM), lambda b, g, i: (b, g, 0, i, 0)),
                  pl.BlockSpec((1, n_cmp, HEAD_DIM), lambda b, g, i: (b * N_KV + g, 0, 0)),
                  pl.BlockSpec((1, HEAD_DIM, n_cmp), lambda b, g, i: (b * N_KV + g, 0, 0)),
                  full_k, full_vt, full_k, full_vt,
                  pl.BlockSpec((1, GATE_ROWS, TQ), lambda b, g, i: (b, 0, i)),
                  pl.BlockSpec((1, TQ, GROUP * HEAD_DIM), lambda b, g, i: (b, i, g))],
        out_specs=pl.BlockSpec((1, TQ, GROUP * HEAD_DIM), lambda b, g, i: (b, i, g)),
        scratch_shapes=[pltpu.VMEM((S // SLC_BLOCK, TQ), jnp.float32)],
        compiler_params=pltpu.CompilerParams(
            dimension_semantics=("parallel", "parallel", "arbitrary"),
            vmem_limit_bytes=VMEM_LIMIT),
        name="nsa",
    )(qa, kc, vct, ks, vst, kw, vwt, gt, sza)


def _swa_kernel(sinks_ref, qb_ref, kb_ref, vbt_ref, szb_ref, o_ref):
    g = pl.program_id(1)
    qi = pl.program_id(2)
    q = qb_ref[0, 0].reshape(GROUP * TQ, HEAD_DIM)
    o_t = _banded_attention(q, kb_ref, vbt_ref, g, qi, SWA_WINDOW,
                            lambda r: sinks_ref[g * GROUP + r])
    _store_heads(o_t, szb_ref, o_ref)


def _swa(sinks, qb, kb, vbt, szb):
    B, _, _, S, _ = qb.shape
    return pl.pallas_call(
        _swa_kernel,
        out_shape=jax.ShapeDtypeStruct((B, S, WIDTH), jnp.bfloat16),
        grid=(B, N_KV, S // TQ),
        in_specs=[pl.BlockSpec(memory_space=pltpu.SMEM),
                  pl.BlockSpec((1, 1, GROUP, TQ, HEAD_DIM), lambda b, g, i: (b, g, 0, i, 0)),
                  pl.BlockSpec((1, 1, S, HEAD_DIM), lambda b, g, i: (b, g, 0, 0)),
                  pl.BlockSpec((1, 1, HEAD_DIM, S), lambda b, g, i: (b, g, 0, 0)),
                  pl.BlockSpec((1, TQ, GROUP * HEAD_DIM), lambda b, g, i: (b, i, g))],
        out_specs=pl.BlockSpec((1, TQ, GROUP * HEAD_DIM), lambda b, g, i: (b, i, g)),
        compiler_params=pltpu.CompilerParams(
            dimension_semantics=("parallel", "parallel", "arbitrary"),
            vmem_limit_bytes=VMEM_LIMIT),
        name="swa",
    )(sinks, qb, kb, vbt, szb)


def _out_proj_kernel(x_ref, oza_ref, ozb_ref, sma_ref, smb_ref, gate_ref, gpost_ref,
                     woa_ref, wob_ref, wout_ref, o_ref):
    ya = _dot(oza_ref[0], woa_ref[...])
    yb = _dot(ozb_ref[0], wob_ref[...])
    y = sma_ref[0].astype(jnp.float32) * ya + smb_ref[0].astype(jnp.float32) * yb
    yo = _dot(y.astype(jnp.bfloat16), wout_ref[...])
    ms = jnp.mean(yo * yo, axis=-1, keepdims=True)
    normed = yo * lax.rsqrt(ms + RMS_EPS) * gpost_ref[...]
    o_ref[0] = x_ref[0] + gate_ref[0] * normed


def _out_proj(x, oza, ozb, sma, smb, gate, gpost, woa, wob, wout):
    B, S, D = x.shape
    tm = TM_PROJ
    row_spec = lambda n: pl.BlockSpec((1, tm, n), lambda b, i: (b, i, 0))
    const = lambda shape: pl.BlockSpec(shape, lambda b, i: (0,) * len(shape))
    return pl.pallas_call(
        _out_proj_kernel,
        out_shape=jax.ShapeDtypeStruct((B, S, D), jnp.float32),
        grid=(B, S // tm),
        in_specs=[row_spec(D), row_spec(WIDTH), row_spec(WIDTH), row_spec(D), row_spec(D),
                  pl.BlockSpec((1, 1, D), lambda b, i: (b, 0, 0)), const((1, D)),
                  const((WIDTH, D)), const((WIDTH, D)), const((D, D))],
        out_specs=row_spec(D),
        compiler_params=pltpu.CompilerParams(
            dimension_semantics=("parallel", "parallel"), vmem_limit_bytes=VMEM_LIMIT),
        name="out_proj",
    )(x, oza, ozb, sma, smb, gate, gpost, woa, wob, wout)


def _arrange_w_in(w):
    o = 0
    seg = {}
    for name, n in (("qa", WIDTH), ("kvc", 2 * KVW), ("kvs", 2 * KVW), ("kvw", 2 * KVW),
                    ("g", N_GATES), ("za", WIDTH), ("qb", WIDTH), ("kvb", 2 * KVW),
                    ("zb", WIDTH), ("m", 2 * D_MODEL)):
        seg[name] = w[:, o:o + n]
        o += n
    scale = HEAD_DIM ** -0.5
    gpad = jnp.pad(seg["g"], ((0, 0), (0, GATE_PAD - N_GATES)))
    cols = [seg["qa"] * scale, seg["qb"] * scale, seg["kvc"], seg["kvs"], seg["kvw"], seg["kvb"],
            seg["za"], seg["zb"], seg["m"], gpad]
    return jnp.concatenate(cols, axis=1).astype(jnp.bfloat16)


def kernel(x, c, w_ada, b_ada, g_pre, g_post, w_in, pe_cmp_k, pe_cmp_v, w_cmp_k1, w_cmp_k2,
           w_cmp_v1, w_cmp_v2, w_o_nsa, w_o_swa, w_out, sinks):
    B, S, D = x.shape
    depth = w_in.shape[0]
    bf = jnp.bfloat16
    for l in range(depth):
        c8 = jnp.pad(c, ((0, 8 - B), (0, 0)))
        mod = _adaln_mod(c8, w_ada[l], b_ada[l][None, :])[:B]
        shift, scale, gate = (mod[:, None, i * D:(i + 1) * D] for i in range(3))

        (qa, qb, kc_raw, vc_raw, ks, vst, kw, vwt, kb, vbt, sza, szb, sma, smb, gt) = _in_proj(
            x, shift, scale, g_pre[l][None, :], _arrange_w_in(w_in[l]))

        rows = S // CMP_STRIDE
        t = jnp.stack([kc_raw, vc_raw]).reshape(2, B * N_KV, rows, CMP_STRIDE * HEAD_DIM)
        pe = jnp.stack([pe_cmp_k[l], pe_cmp_v[l]]).reshape(2, 1, CMP_BLOCK * HEAD_DIM)
        cmp_o, cmp_ot = _compress(t, pe, jnp.stack([w_cmp_k1[l], w_cmp_v1[l]]),
                                  jnp.stack([w_cmp_k2[l], w_cmp_v2[l]]))

        oza = _nsa(qa, cmp_o[0], cmp_ot[1], ks, vst, kw, vwt, gt, sza)
        ozb = _swa(sinks[l], qb, kb, vbt, szb)

        x = _out_proj(x, oza, ozb, sma, smb, gate, g_post[l][None, :],
                      w_o_nsa[l].astype(bf), w_o_swa[l].astype(bf), w_out[l].astype(bf))
    return x
```

```python
import jax
import jax.numpy as jnp
from jax import lax
from jax.experimental import pallas as pl
from jax.experimental.pallas import tpu as pltpu

D_MODEL = 1024
HEAD_DIM = 64
N_HEADS = 8
N_KV = 2
GROUP = N_HEADS // N_KV
CMP_BLOCK = 32
CMP_STRIDE = 16
CMP_HIDDEN = 256
SLC_BLOCK = 64
N_SELECT = 16
NSA_WINDOW = 512
SWA_WINDOW = 128
RMS_EPS = 1e-6
NEG = -1e30
BIG = 1e30
M_FLOOR = -1e20
NEG_FEATURE = -(2.0 ** 100)
PAD_KEY_BLOCK = -(2.0 ** 90)

WIDTH = N_HEADS * HEAD_DIM
KVW = N_KV * HEAD_DIM
FEAT = 128
LANE_BLK = HEAD_DIM
LANE_POS = HEAD_DIM + 1
N_GATES = 3 * N_HEADS
GATE_PAD = 128
GATE_ROWS = 32

VMEM_LIMIT = 56 * 1024 * 1024

TM_PROJ = 512
TQ = 128
TK_SLC = 512
SUB = 128

_NT = (((1,), (1,)), ((), ()))


def _dot(a, b):
    return jnp.dot(a, b, preferred_element_type=jnp.float32)


def _dot_nt(a, b):
    return lax.dot_general(a, b, _NT, preferred_element_type=jnp.float32)


def _sigmoid(v):
    return 1.0 / (1.0 + jnp.exp(-v))


def _iota(shape, dim):
    return lax.broadcasted_iota(jnp.int32, shape, dim)


def _adaln_kernel(c_ref, w_ref, b_ref, o_ref):
    o_ref[...] = jnp.dot(c_ref[...], w_ref[...], preferred_element_type=jnp.float32,
                         precision=lax.Precision.HIGHEST) + b_ref[...]


def _adaln_mod(c8, w, b):
    n = w.shape[1]
    bn = 1024
    return pl.pallas_call(
        _adaln_kernel,
        out_shape=jax.ShapeDtypeStruct((c8.shape[0], n), jnp.float32),
        grid=(n // bn,),
        in_specs=[pl.BlockSpec((c8.shape[0], D_MODEL), lambda j: (0, 0)),
                  pl.BlockSpec((D_MODEL, bn), lambda j: (0, j)),
                  pl.BlockSpec((1, bn), lambda j: (0, j))],
        out_specs=pl.BlockSpec((c8.shape[0], bn), lambda j: (0, j)),
        name="adaln_mod",
    )(c8, w, b)


_C_Q = 0
_C_KV = 2 * WIDTH
_C_Z = _C_KV + 8 * KVW
_C_M = _C_Z + 2 * WIDTH
_C_G = _C_M + 2 * D_MODEL
_C_END = _C_G + GATE_PAD


def _with_features(pair, odd, feat, lane):
    src = pltpu.roll(pair, HEAD_DIM, 1) if odd else pair
    return jnp.where(lane < HEAD_DIM, src, feat).astype(jnp.bfloat16)


def _in_proj_kernel(x_ref, shift_ref, scale_ref, gpre_ref, w_ref,
                    qa_ref, qb_ref, kc_ref, vc_ref, ks_ref, vst_ref, kw_ref, vwt_ref,
                    kb_ref, vbt_ref, sza_ref, szb_ref, sma_ref, smb_ref, gt_ref):
    tm = x_ref.shape[1]
    xf = x_ref[0]
    ms = jnp.mean(xf * xf, axis=-1, keepdims=True)
    y = xf * lax.rsqrt(ms + RMS_EPS) * gpre_ref[...]
    h = y * (1.0 + scale_ref[0]) + shift_ref[0]
    hb = h.astype(jnp.bfloat16)

    lane = _iota((tm, FEAT), 1)
    pos = pl.program_id(1) * tm + _iota((tm, FEAT), 0)
    key_feat = jnp.where(lane == LANE_BLK, lax.shift_right_logical(pos, 6).astype(jnp.float32),
                         jnp.where(lane == LANE_POS, (pos & (SLC_BLOCK - 1)).astype(jnp.float32), 0.0))

    acc = _dot(hb, w_ref[:, _C_Q:_C_KV])
    for head in range(N_HEADS):
        slope = 2.0 ** -(head + 1)
        q_feat = jnp.where(lane == LANE_BLK, slope * SLC_BLOCK,
                           jnp.where(lane == LANE_POS, slope, 0.0))
        g, r, odd = head // GROUP, head % GROUP, head % 2
        c = (head // 2) * FEAT
        qa_ref[0, g, r] = _with_features(acc[:, c:c + FEAT], odd, q_feat, lane)
        qb_ref[0, g, r] = _with_features(acc[:, WIDTH + c:WIDTH + c + FEAT], odd, q_feat, lane)

    acc = _dot(hb, w_ref[:, _C_KV:_C_Z])
    for g in range(N_KV):
        c = g * HEAD_DIM
        kc_ref[0, g] = acc[:, c:c + HEAD_DIM].astype(jnp.bfloat16)
        vc_ref[0, g] = acc[:, KVW + c:KVW + c + HEAD_DIM].astype(jnp.bfloat16)
        ks_ref[0, g] = _with_features(acc[:, 2 * KVW:3 * KVW], g, key_feat, lane)
        kw_ref[0, g] = _with_features(acc[:, 4 * KVW:5 * KVW], g, key_feat, lane)
        kb_ref[0, g] = _with_features(acc[:, 6 * KVW:7 * KVW], g, key_feat, lane)
    vst = acc[:, 3 * KVW:4 * KVW].T
    vwt = acc[:, 5 * KVW:6 * KVW].T
    vbt = acc[:, 7 * KVW:8 * KVW].T
    for g in range(N_KV):
        c = g * HEAD_DIM
        vst_ref[0, g] = vst[c:c + HEAD_DIM].astype(jnp.bfloat16)
        vwt_ref[0, g] = vwt[c:c + HEAD_DIM].astype(jnp.bfloat16)
        vbt_ref[0, g] = vbt[c:c + HEAD_DIM].astype(jnp.bfloat16)

    acc = _dot(hb, w_ref[:, _C_Z:_C_M])
    sz = acc * _sigmoid(acc)
    sza_ref[0] = sz[:, :WIDTH].astype(jnp.bfloat16)
    szb_ref[0] = sz[:, WIDTH:].astype(jnp.bfloat16)

    acc = _dot(hb, w_ref[:, _C_M:_C_M + D_MODEL])
    sma_ref[0] = _sigmoid(acc).astype(jnp.bfloat16)
    acc = _dot(hb, w_ref[:, _C_M + D_MODEL:_C_G])
    smb_ref[0] = _sigmoid(acc).astype(jnp.bfloat16)

    acc = _dot(hb, w_ref[:, _C_G:_C_END])
    gt_ref[0] = _sigmoid(acc).T[:GATE_ROWS]


def _in_proj(x, shift, scale, gpre, w):
    B, S, D = x.shape
    tm = TM_PROJ
    bf = jnp.bfloat16
    q_shape = jax.ShapeDtypeStruct((B, N_KV, GROUP, S, FEAT), bf)
    kraw_shape = jax.ShapeDtypeStruct((B, N_KV, S, HEAD_DIM), bf)
    k_shape = jax.ShapeDtypeStruct((B, N_KV, S, FEAT), bf)
    vt_shape = jax.ShapeDtypeStruct((B, N_KV, HEAD_DIM, S), bf)
    q_spec = pl.BlockSpec((1, N_KV, GROUP, tm, FEAT), lambda b, i: (b, 0, 0, i, 0))
    kraw_spec = pl.BlockSpec((1, N_KV, tm, HEAD_DIM), lambda b, i: (b, 0, i, 0))
    k_spec = pl.BlockSpec((1, N_KV, tm, FEAT), lambda b, i: (b, 0, i, 0))
    vt_spec = pl.BlockSpec((1, N_KV, HEAD_DIM, tm), lambda b, i: (b, 0, 0, i))
    row_spec = lambda n: pl.BlockSpec((1, tm, n), lambda b, i: (b, i, 0))
    vec_spec = pl.BlockSpec((1, 1, D), lambda b, i: (b, 0, 0))
    return pl.pallas_call(
        _in_proj_kernel,
        out_shape=(q_shape, q_shape, kraw_shape, kraw_shape, k_shape, vt_shape, k_shape, vt_shape,
                   k_shape, vt_shape,
                   jax.ShapeDtypeStruct((B, S, WIDTH), bf), jax.ShapeDtypeStruct((B, S, WIDTH), bf),
                   jax.ShapeDtypeStruct((B, S, D), bf), jax.ShapeDtypeStruct((B, S, D), bf),
                   jax.ShapeDtypeStruct((B, GATE_ROWS, S), jnp.float32)),
        grid=(B, S // tm),
        in_specs=[row_spec(D), vec_spec, vec_spec,
                  pl.BlockSpec((1, D), lambda b, i: (0, 0)),
                  pl.BlockSpec((D, _C_END), lambda b, i: (0, 0), pipeline_mode=pl.Buffered(1))],
        out_specs=(q_spec, q_spec, kraw_spec, kraw_spec, k_spec, vt_spec, k_spec, vt_spec,
                   k_spec, vt_spec,
                   row_spec(WIDTH), row_spec(WIDTH), row_spec(D), row_spec(D),
                   pl.BlockSpec((1, GATE_ROWS, tm), lambda b, i: (b, 0, i))),
        compiler_params=pltpu.CompilerParams(
            dimension_semantics=("parallel", "parallel"), vmem_limit_bytes=VMEM_LIMIT),
        name="in_proj",
    )(x, shift, scale, gpre, w)


def _compress_kernel(t_ref, pe_ref, w1_ref, w2_ref, o_ref, ot_ref):
    half = CMP_STRIDE * HEAD_DIM
    t = t_ref[0, 0]
    w1 = w1_ref[0]
    w1b = w1.astype(jnp.bfloat16)
    first = _dot(t, w1b[:half])
    second = _dot(t, w1b[half:])
    n = second.shape[0]
    second = pltpu.roll(second, n - 1, 0)
    pe = jnp.broadcast_to(pe_ref[0], (8, 2 * half))
    bias = jnp.dot(pe, w1, preferred_element_type=jnp.float32,
                   precision=lax.Precision.HIGHEST)[0:1]
    pre = first + second + bias
    hid = pre * _sigmoid(pre)
    out = _dot(hid.astype(jnp.bfloat16), w2_ref[0].astype(jnp.bfloat16))
    lane = _iota((n, FEAT), 1)
    c = _iota((n, FEAT), 0)
    end = c * CMP_STRIDE + (CMP_BLOCK - 1)
    feat = jnp.where(lane == LANE_BLK, lax.shift_right_logical(end, 6).astype(jnp.float32),
                     jnp.where(lane == LANE_POS, (end & (SLC_BLOCK - 1)).astype(jnp.float32), 0.0))
    o_ref[0, 0] = jnp.where(lane < HEAD_DIM, out, feat).astype(jnp.bfloat16)
    ot_ref[0, 0] = out.T[:HEAD_DIM].astype(jnp.bfloat16)


def _compress(t, pe, w1, w2):
    _, BG, n, wdt = t.shape
    return pl.pallas_call(
        _compress_kernel,
        out_shape=(jax.ShapeDtypeStruct((2, BG, n, FEAT), jnp.bfloat16),
                   jax.ShapeDtypeStruct((2, BG, HEAD_DIM, n), jnp.bfloat16)),
        grid=(2, BG),
        in_specs=[pl.BlockSpec((1, 1, n, wdt), lambda a, i: (a, i, 0, 0)),
                  pl.BlockSpec((1, 1, 2 * wdt), lambda a, i: (a, 0, 0)),
                  pl.BlockSpec((1, 2 * wdt, CMP_HIDDEN), lambda a, i: (a, 0, 0)),
                  pl.BlockSpec((1, CMP_HIDDEN, FEAT), lambda a, i: (a, 0, 0))],
        out_specs=(pl.BlockSpec((1, 1, n, FEAT), lambda a, i: (a, i, 0, 0)),
                   pl.BlockSpec((1, 1, HEAD_DIM, n), lambda a, i: (a, i, 0, 0))),
        compiler_params=pltpu.CompilerParams(dimension_semantics=("parallel", "parallel")),
        name="compress",
    )(t, pe, w1, w2)


def _head_slope(g, r):
    return jnp.where(g == 0, 1.0, 2.0 ** -GROUP).astype(jnp.float32) * (2.0 ** -(r + 1))


def _per_head(x):
    return jnp.concatenate([x] * GROUP, axis=1)


def _banded_attention(q, k_ref, vt_ref, q0, window, sink_row):
    n_keys = window + TQ
    start = pl.multiple_of(q0, TQ)
    st = _dot_nt(k_ref[0, 0, pl.ds(start, n_keys), :], q)
    row, col = _iota((SUB, TQ), 0), _iota((SUB, TQ), 1)
    too_old = _per_head(jnp.where(row > col, 0.0, NEG))
    future = _per_head(jnp.where(row <= col, 0.0, NEG))
    middle = [st[SUB:n_keys - SUB]] if n_keys > 2 * SUB else []
    s = jnp.concatenate([st[:SUB] + too_old] + middle + [st[n_keys - SUB:] + future], axis=0)
    m = jnp.max(s, axis=0, keepdims=True)
    if sink_row is not None:
        m = jnp.maximum(m, sink_row)
    p = jnp.exp(s - m)
    l = jnp.sum(p, axis=0, keepdims=True)
    if sink_row is not None:
        l = l + jnp.exp(sink_row - m)
    o = _dot(vt_ref[0, 0, :, pl.ds(start, n_keys)], p.astype(jnp.bfloat16))
    return o / l


def _store_heads(o_t, sz_ref, o_ref):
    stacked = jnp.concatenate([o_t[:, r * TQ:(r + 1) * TQ] for r in range(GROUP)], axis=0)
    o_ref[0] = (stacked.T * sz_ref[0].astype(jnp.float32)).astype(jnp.bfloat16)


def _importance_matrix(n_cmp):
    ratio = SLC_BLOCK // CMP_STRIDE
    span = CMP_BLOCK // CMP_STRIDE
    j = _iota((SLC_BLOCK, n_cmp), 0)
    c = _iota((SLC_BLOCK, n_cmp), 1)
    off = ratio * j - c
    w = jnp.zeros((SLC_BLOCK, n_cmp), jnp.float32)
    for mm in range(ratio):
        for nn in range(span):
            w = w + jnp.where(off == mm + nn, 1.0, 0.0)
    w = jnp.where(c < n_cmp - 1, w, 0.0)
    return w.astype(jnp.bfloat16)


def _selection_rank(score):
    nsb = score.shape[0]
    groups = [score[8 * a:8 * a + 8] for a in range(nsb // 8)]
    sub = _iota((8, TQ), 0)
    rank = [jnp.zeros((8, TQ), jnp.float32) for _ in groups]
    for i in range(nsb):
        row = jnp.broadcast_to(score[i:i + 1, :], (8, TQ))
        for a, grp in enumerate(groups):
            if a < i // 8:
                ahead = jnp.where(row > grp, 1.0, 0.0)
            elif a > i // 8:
                ahead = jnp.where(row >= grp, 1.0, 0.0)
            else:
                ahead = jnp.where(sub > i % 8, jnp.where(row >= grp, 1.0, 0.0),
                                  jnp.where(row > grp, 1.0, 0.0))
            rank[a] = rank[a] + ahead
    return jnp.concatenate(rank, axis=0)


def _nsa_kernel(qa_ref, kc_ref, vct_ref, ks_ref, vst_ref, kw_ref, vwt_ref, onehot_ref, gt_ref,
                sza_ref, o_ref):
    g = pl.program_id(1)
    qi = pl.program_id(2)
    q0 = qi * TQ
    q = qa_ref[0, 0].reshape(GROUP * TQ, FEAT)

    n_cmp = kc_ref.shape[1]
    st = _dot_nt(kc_ref[0], q)
    qpos_c = q0 + _iota((n_cmp, TQ), 1)
    end_c = _iota((n_cmp, TQ), 0) * CMP_STRIDE + (CMP_BLOCK - 1)
    hidden = jnp.where(qpos_c >= end_c, 0.0, NEG)
    any_valid = (qpos_c[0:1] >= CMP_BLOCK - 1).astype(jnp.float32)
    p_sum = jnp.zeros((n_cmp, TQ), jnp.float32)
    o_cmp = []
    for r in range(GROUP):
        sr = st[:, r * TQ:(r + 1) * TQ] + hidden
        er = jnp.exp(sr - jnp.max(sr, axis=0, keepdims=True))
        inv = any_valid / jnp.sum(er, axis=0, keepdims=True)
        p_sum = p_sum + er * inv
        o_cmp.append(_dot(vct_ref[0], er.astype(jnp.bfloat16)) * inv)
    o_cmp = jnp.concatenate(o_cmp, axis=1)

    wimp = _importance_matrix(n_cmp)
    p_hi = p_sum.astype(jnp.bfloat16)
    rem = p_sum - p_hi.astype(jnp.float32)
    p_mid = rem.astype(jnp.bfloat16)
    p_lo = (rem - p_mid.astype(jnp.float32)).astype(jnp.bfloat16)
    imp = _dot(wimp, p_hi) + _dot(wimp, p_mid) + _dot(wimp, p_lo)

    nsb = imp.shape[0]
    jrow = _iota((nsb, TQ), 0)
    cur = lax.shift_right_logical(q0 + _iota((nsb, TQ), 1), 6)
    causal = jrow <= cur
    forced = (jrow == 0) | (jrow == cur) | (jrow == cur - 1)
    score = jnp.where(causal, jnp.where(forced, BIG, imp), NEG)
    rank = _selection_rank(score)
    unselected = jnp.where(causal, jnp.where(rank < float(N_SELECT), 0.0, NEG_FEATURE), NEG_FEATURE)
    mask_feat = jnp.concatenate([unselected, jnp.zeros((FEAT - nsb, TQ), jnp.float32)], axis=0).T
    q_sel = jnp.concatenate(
        [q, jnp.concatenate([mask_feat.astype(jnp.bfloat16)] * GROUP, axis=0)], axis=1)

    tk = TK_SLC

    def slc_tile(j, carry, on_diagonal):
        m, l, acc = carry
        k0 = pl.multiple_of(j * tk, tk)
        k_aug = jnp.concatenate([ks_ref[0, 0, pl.ds(k0, tk), :], onehot_ref[pl.ds(k0, tk), :]], axis=1)
        s = _dot_nt(k_aug, q_sel)
        if on_diagonal:
            dist = (q0 + _iota((tk, TQ), 1)) - (k0 + _iota((tk, TQ), 0))
            s = s + _per_head(jnp.where(dist >= 0, 0.0, NEG))
        m_new = jnp.maximum(m, jnp.max(s, axis=0, keepdims=True))
        alpha = jnp.exp(m - m_new)
        p = jnp.exp(s - m_new)
        l = alpha * l + jnp.sum(p, axis=0, keepdims=True)
        acc = alpha * acc + _dot(vst_ref[0, 0, :, pl.ds(k0, tk)], p.astype(jnp.bfloat16))
        return m_new, l, acc

    n = GROUP * TQ
    init = (jnp.full((1, n), M_FLOOR, jnp.float32), jnp.zeros((1, n), jnp.float32),
            jnp.zeros((HEAD_DIM, n), jnp.float32))
    last = (q0 + TQ - 1) // tk
    carry = lax.fori_loop(0, last, lambda j, c: slc_tile(j, c, False), init)
    _, l, acc = slc_tile(last, carry, True)
    o_slc = acc / l

    o_win = _banded_attention(q, kw_ref, vwt_ref, q0, NSA_WINDOW, None)

    pieces = []
    for r in range(GROUP):
        sl = slice(r * TQ, (r + 1) * TQ)
        gate = [gt_ref[0, pl.ds(br * N_HEADS + g * GROUP + r, 1), :] for br in range(3)]
        pieces.append(gate[0] * o_cmp[:, sl] + gate[1] * o_slc[:, sl] + gate[2] * o_win[:, sl])
    _store_heads(jnp.concatenate(pieces, axis=1), sza_ref, o_ref)


def _nsa(qa, kc, vct, ks, vst, kw, vwt, onehot, gt, sza):
    B, _, _, S, _ = qa.shape
    n_cmp = kc.shape[1]
    sp = kw.shape[2]
    grid = (B, N_KV, S // TQ)
    return pl.pallas_call(
        _nsa_kernel,
        out_shape=jax.ShapeDtypeStruct((B, S, WIDTH), jnp.bfloat16),
        grid=grid,
        in_specs=[pl.BlockSpec((1, 1, GROUP, TQ, FEAT), lambda b, g, i: (b, g, 0, i, 0)),
                  pl.BlockSpec((1, n_cmp, FEAT), lambda b, g, i: (b * N_KV + g, 0, 0)),
                  pl.BlockSpec((1, HEAD_DIM, n_cmp), lambda b, g, i: (b * N_KV + g, 0, 0)),
                  pl.BlockSpec((1, 1, S, FEAT), lambda b, g, i: (b, g, 0, 0)),
                  pl.BlockSpec((1, 1, HEAD_DIM, S), lambda b, g, i: (b, g, 0, 0)),
                  pl.BlockSpec((1, 1, sp, FEAT), lambda b, g, i: (b, g, 0, 0)),
                  pl.BlockSpec((1, 1, HEAD_DIM, sp), lambda b, g, i: (b, g, 0, 0)),
                  pl.BlockSpec((S, FEAT), lambda b, g, i: (0, 0)),
                  pl.BlockSpec((1, GATE_ROWS, TQ), lambda b, g, i: (b, 0, i)),
                  pl.BlockSpec((1, TQ, GROUP * HEAD_DIM), lambda b, g, i: (b, i, g))],
        out_specs=pl.BlockSpec((1, TQ, GROUP * HEAD_DIM), lambda b, g, i: (b, i, g)),
        compiler_params=pltpu.CompilerParams(
            dimension_semantics=("parallel", "parallel", "arbitrary"),
            vmem_limit_bytes=VMEM_LIMIT),
        name="nsa",
    )(qa, kc, vct, ks, vst, kw, vwt, onehot, gt, sza)


def _swa_kernel(sinks_ref, qb_ref, kb_ref, vbt_ref, szb_ref, o_ref):
    g = pl.program_id(1)
    q0 = pl.program_id(2) * TQ
    q = qb_ref[0, 0].reshape(GROUP * TQ, FEAT)
    qpos = (q0 + _iota((1, TQ), 1)).astype(jnp.float32)
    sink_row = jnp.concatenate(
        [sinks_ref[g * GROUP + r] + _head_slope(g, r) * qpos for r in range(GROUP)], axis=1)
    o_t = _banded_attention(q, kb_ref, vbt_ref, q0, SWA_WINDOW, sink_row)
    _store_heads(o_t, szb_ref, o_ref)


def _swa(sinks, qb, kb, vbt, szb):
    B, _, _, S, _ = qb.shape
    sp = kb.shape[2]
    return pl.pallas_call(
        _swa_kernel,
        out_shape=jax.ShapeDtypeStruct((B, S, WIDTH), jnp.bfloat16),
        grid=(B, N_KV, S // TQ),
        in_specs=[pl.BlockSpec(memory_space=pltpu.SMEM),
                  pl.BlockSpec((1, 1, GROUP, TQ, FEAT), lambda b, g, i: (b, g, 0, i, 0)),
                  pl.BlockSpec((1, 1, sp, FEAT), lambda b, g, i: (b, g, 0, 0)),
                  pl.BlockSpec((1, 1, HEAD_DIM, sp), lambda b, g, i: (b, g, 0, 0)),
                  pl.BlockSpec((1, TQ, GROUP * HEAD_DIM), lambda b, g, i: (b, i, g))],
        out_specs=pl.BlockSpec((1, TQ, GROUP * HEAD_DIM), lambda b, g, i: (b, i, g)),
        compiler_params=pltpu.CompilerParams(
            dimension_semantics=("parallel", "parallel", "arbitrary"),
            vmem_limit_bytes=VMEM_LIMIT),
        name="swa",
    )(sinks, qb, kb, vbt, szb)


def _out_proj_kernel(x_ref, oza_ref, ozb_ref, sma_ref, smb_ref, gate_ref, gpost_ref,
                     woa_ref, wob_ref, wout_ref, o_ref):
    ya = _dot(oza_ref[0], woa_ref[...])
    yb = _dot(ozb_ref[0], wob_ref[...])
    y = sma_ref[0].astype(jnp.float32) * ya + smb_ref[0].astype(jnp.float32) * yb
    yo = _dot(y.astype(jnp.bfloat16), wout_ref[...])
    ms = jnp.mean(yo * yo, axis=-1, keepdims=True)
    normed = yo * lax.rsqrt(ms + RMS_EPS) * gpost_ref[...]
    o_ref[0] = x_ref[0] + gate_ref[0] * normed


def _out_proj(x, oza, ozb, sma, smb, gate, gpost, woa, wob, wout):
    B, S, D = x.shape
    tm = TM_PROJ
    row_spec = lambda n: pl.BlockSpec((1, tm, n), lambda b, i: (b, i, 0))
    const = lambda shape: pl.BlockSpec(shape, lambda b, i: (0,) * len(shape))
    return pl.pallas_call(
        _out_proj_kernel,
        out_shape=jax.ShapeDtypeStruct((B, S, D), jnp.float32),
        grid=(B, S // tm),
        in_specs=[row_spec(D), row_spec(WIDTH), row_spec(WIDTH), row_spec(D), row_spec(D),
                  pl.BlockSpec((1, 1, D), lambda b, i: (b, 0, 0)), const((1, D)),
                  const((WIDTH, D)), const((WIDTH, D)), const((D, D))],
        out_specs=row_spec(D),
        compiler_params=pltpu.CompilerParams(
            dimension_semantics=("parallel", "parallel"), vmem_limit_bytes=VMEM_LIMIT),
        name="out_proj",
    )(x, oza, ozb, sma, smb, gate, gpost, woa, wob, wout)


def _arrange_w_in(w):
    o = 0
    seg = {}
    for name, n in (("qa", WIDTH), ("kvc", 2 * KVW), ("kvs", 2 * KVW), ("kvw", 2 * KVW),
                    ("g", N_GATES), ("za", WIDTH), ("qb", WIDTH), ("kvb", 2 * KVW),
                    ("zb", WIDTH), ("m", 2 * D_MODEL)):
        seg[name] = w[:, o:o + n]
        o += n
    scale = HEAD_DIM ** -0.5
    gpad = jnp.pad(seg["g"], ((0, 0), (0, GATE_PAD - N_GATES)))
    cols = [seg["qa"] * scale, seg["qb"] * scale, seg["kvc"], seg["kvs"], seg["kvw"], seg["kvb"],
            seg["za"], seg["zb"], seg["m"], gpad]
    return jnp.concatenate(cols, axis=1).astype(jnp.bfloat16)


def _front_pad(k, vt, n):
    B, G = k.shape[:2]
    pad_row = jnp.zeros((FEAT,), jnp.float32).at[LANE_BLK].set(PAD_KEY_BLOCK).astype(k.dtype)
    k_pad = jnp.broadcast_to(pad_row, (B, G, n, FEAT))
    vt_pad = jnp.zeros((B, G, HEAD_DIM, n), vt.dtype)
    return jnp.concatenate([k_pad, k], axis=2), jnp.concatenate([vt_pad, vt], axis=3)


def _block_onehot(S):
    blk = jnp.arange(S, dtype=jnp.int32)[:, None] // SLC_BLOCK
    return (blk == jnp.arange(FEAT, dtype=jnp.int32)[None, :]).astype(jnp.bfloat16)


def kernel(x, c, w_ada, b_ada, g_pre, g_post, w_in, pe_cmp_k, pe_cmp_v, w_cmp_k1, w_cmp_k2,
           w_cmp_v1, w_cmp_v2, w_o_nsa, w_o_swa, w_out, sinks):
    B, S, D = x.shape
    depth = w_in.shape[0]
    bf = jnp.bfloat16
    for l in range(depth):
        c8 = jnp.pad(c, ((0, 8 - B), (0, 0)))
        mod = _adaln_mod(c8, w_ada[l], b_ada[l][None, :])[:B]
        shift, scale, gate = (mod[:, None, i * D:(i + 1) * D] for i in range(3))

        (qa, qb, kc_raw, vc_raw, ks, vst, kw, vwt, kb, vbt, sza, szb, sma, smb, gt) = _in_proj(
            x, shift, scale, g_pre[l][None, :], _arrange_w_in(w_in[l]))

        rows = S // CMP_STRIDE
        t = jnp.stack([kc_raw, vc_raw]).reshape(2, B * N_KV, rows, CMP_STRIDE * HEAD_DIM)
        pe = jnp.stack([pe_cmp_k[l], pe_cmp_v[l]]).reshape(2, 1, CMP_BLOCK * HEAD_DIM)
        w2 = jnp.pad(jnp.stack([w_cmp_k2[l], w_cmp_v2[l]]), ((0, 0), (0, 0), (0, FEAT - HEAD_DIM)))
        cmp_o, cmp_ot = _compress(t, pe, jnp.stack([w_cmp_k1[l], w_cmp_v1[l]]), w2)

        kw, vwt = _front_pad(kw, vwt, NSA_WINDOW)
        kb, vbt = _front_pad(kb, vbt, SWA_WINDOW)
        oza = _nsa(qa, cmp_o[0], cmp_ot[1], ks, vst, kw, vwt, _block_onehot(S), gt, sza)
        ozb = _swa(sinks[l], qb, kb, vbt, szb)

        x = _out_proj(x, oza, ozb, sma, smb, gate, g_post[l][None, :],
                      w_o_nsa[l].astype(bf), w_o_swa[l].astype(bf), w_out[l].astype(bf))
    return x
```

```python
import jax
import jax.numpy as jnp
from jax import lax
from jax.experimental import pallas as pl
from jax.experimental.pallas import tpu as pltpu

D_MODEL = 1024
HEAD_DIM = 64
N_HEADS = 8
N_KV = 2
GROUP = N_HEADS // N_KV
CMP_BLOCK = 32
CMP_STRIDE = 16
CMP_HIDDEN = 256
SLC_BLOCK = 64
N_SELECT = 16
NSA_WINDOW = 512
SWA_WINDOW = 128
RMS_EPS = 1e-6
NEG = -1e30
BIG = 1e30
M_FLOOR = -1e20
NEG_FEATURE = -(2.0 ** 100)
PAD_KEY_BLOCK = -(2.0 ** 90)

WIDTH = N_HEADS * HEAD_DIM
KVW = N_KV * HEAD_DIM
FEAT = 128
LANE_BLK = HEAD_DIM
LANE_POS = HEAD_DIM + 1
N_GATES = 3 * N_HEADS
GATE_PAD = 128
GATE_ROWS = 32

VMEM_LIMIT = 56 * 1024 * 1024

TM_PROJ = 512
TQ = 128
TK_SLC = 512
SUB = 128

_NT = (((1,), (1,)), ((), ()))


def _dot(a, b):
    return jnp.dot(a, b, preferred_element_type=jnp.float32)


def _dot_nt(a, b):
    return lax.dot_general(a, b, _NT, preferred_element_type=jnp.float32)


def _sigmoid(v):
    return 1.0 / (1.0 + jnp.exp(-v))


def _iota(shape, dim):
    return lax.broadcasted_iota(jnp.int32, shape, dim)


def _adaln_kernel(c_ref, w_ref, b_ref, o_ref):
    o_ref[...] = jnp.dot(c_ref[...], w_ref[...], preferred_element_type=jnp.float32,
                         precision=lax.Precision.HIGHEST) + b_ref[...]


def _adaln_mod(c8, w, b):
    n = w.shape[1]
    bn = 1024
    return pl.pallas_call(
        _adaln_kernel,
        out_shape=jax.ShapeDtypeStruct((c8.shape[0], n), jnp.float32),
        grid=(n // bn,),
        in_specs=[pl.BlockSpec((c8.shape[0], D_MODEL), lambda j: (0, 0)),
                  pl.BlockSpec((D_MODEL, bn), lambda j: (0, j)),
                  pl.BlockSpec((1, bn), lambda j: (0, j))],
        out_specs=pl.BlockSpec((c8.shape[0], bn), lambda j: (0, j)),
        name="adaln_mod",
    )(c8, w, b)


_C_Q = 0
_C_KV = 2 * WIDTH
_C_Z = _C_KV + 8 * KVW
_C_M = _C_Z + 2 * WIDTH
_C_G = _C_M + 2 * D_MODEL
_C_END = _C_G + GATE_PAD


def _with_features(pair, odd, feat, lane):
    src = pltpu.roll(pair, HEAD_DIM, 1) if odd else pair
    return jnp.where(lane < HEAD_DIM, src, feat).astype(jnp.bfloat16)


def _in_proj_kernel(x_ref, shift_ref, scale_ref, gpre_ref, w_ref,
                    qa_ref, qb_ref, kc_ref, vc_ref, ks_ref, vst_ref, kw_ref, vwt_ref,
                    kb_ref, vbt_ref, sza_ref, szb_ref, sma_ref, smb_ref, gt_ref):
    tm = x_ref.shape[1]
    xf = x_ref[0]
    ms = jnp.mean(xf * xf, axis=-1, keepdims=True)
    y = xf * lax.rsqrt(ms + RMS_EPS) * gpre_ref[...]
    h = y * (1.0 + scale_ref[0]) + shift_ref[0]
    hb = h.astype(jnp.bfloat16)

    lane = _iota((tm, FEAT), 1)
    pos = pl.program_id(1) * tm + _iota((tm, FEAT), 0)
    key_feat = jnp.where(lane == LANE_BLK, lax.shift_right_logical(pos, 6).astype(jnp.float32),
                         jnp.where(lane == LANE_POS, (pos & (SLC_BLOCK - 1)).astype(jnp.float32), 0.0))

    acc = _dot(hb, w_ref[:, _C_Q:_C_KV])
    for head in range(N_HEADS):
        slope = 2.0 ** -(head + 1)
        q_feat = jnp.where(lane == LANE_BLK, slope * SLC_BLOCK,
                           jnp.where(lane == LANE_POS, slope, 0.0))
        g, r, odd = head // GROUP, head % GROUP, head % 2
        c = (head // 2) * FEAT
        qa_ref[0, g, r] = _with_features(acc[:, c:c + FEAT], odd, q_feat, lane)
        qb_ref[0, g, r] = _with_features(acc[:, WIDTH + c:WIDTH + c + FEAT], odd, q_feat, lane)

    acc = _dot(hb, w_ref[:, _C_KV:_C_Z])
    for g in range(N_KV):
        c = g * HEAD_DIM
        kc_ref[0, g] = acc[:, c:c + HEAD_DIM].astype(jnp.bfloat16)
        vc_ref[0, g] = acc[:, KVW + c:KVW + c + HEAD_DIM].astype(jnp.bfloat16)
        ks_ref[0, g] = _with_features(acc[:, 2 * KVW:3 * KVW], g, key_feat, lane)
        kw_ref[0, g] = _with_features(acc[:, 4 * KVW:5 * KVW], g, key_feat, lane)
        kb_ref[0, g] = _with_features(acc[:, 6 * KVW:7 * KVW], g, key_feat, lane)
    vst = acc[:, 3 * KVW:4 * KVW].T
    vwt = acc[:, 5 * KVW:6 * KVW].T
    vbt = acc[:, 7 * KVW:8 * KVW].T
    for g in range(N_KV):
        c = g * HEAD_DIM
        vst_ref[0, g] = vst[c:c + HEAD_DIM].astype(jnp.bfloat16)
        vwt_ref[0, g] = vwt[c:c + HEAD_DIM].astype(jnp.bfloat16)
        vbt_ref[0, g] = vbt[c:c + HEAD_DIM].astype(jnp.bfloat16)

    acc = _dot(hb, w_ref[:, _C_Z:_C_M])
    sz = acc * _sigmoid(acc)
    sza_ref[0] = sz[:, :WIDTH].astype(jnp.bfloat16)
    szb_ref[0] = sz[:, WIDTH:].astype(jnp.bfloat16)

    acc = _dot(hb, w_ref[:, _C_M:_C_M + D_MODEL])
    sma_ref[0] = _sigmoid(acc).astype(jnp.bfloat16)
    acc = _dot(hb, w_ref[:, _C_M + D_MODEL:_C_G])
    smb_ref[0] = _sigmoid(acc).astype(jnp.bfloat16)

    acc = _dot(hb, w_ref[:, _C_G:_C_END])
    gt_ref[0] = _sigmoid(acc).T[:GATE_ROWS]


def _in_proj(x, shift, scale, gpre, w):
    B, S, D = x.shape
    tm = TM_PROJ
    bf = jnp.bfloat16
    q_shape = jax.ShapeDtypeStruct((B, N_KV, GROUP, S, FEAT), bf)
    kraw_shape = jax.ShapeDtypeStruct((B, N_KV, S, HEAD_DIM), bf)
    k_shape = jax.ShapeDtypeStruct((B, N_KV, S, FEAT), bf)
    vt_shape = jax.ShapeDtypeStruct((B, N_KV, HEAD_DIM, S), bf)
    q_spec = pl.BlockSpec((1, N_KV, GROUP, tm, FEAT), lambda b, i: (b, 0, 0, i, 0))
    kraw_spec = pl.BlockSpec((1, N_KV, tm, HEAD_DIM), lambda b, i: (b, 0, i, 0))
    k_spec = pl.BlockSpec((1, N_KV, tm, FEAT), lambda b, i: (b, 0, i, 0))
    vt_spec = pl.BlockSpec((1, N_KV, HEAD_DIM, tm), lambda b, i: (b, 0, 0, i))
    row_spec = lambda n: pl.BlockSpec((1, tm, n), lambda b, i: (b, i, 0))
    vec_spec = pl.BlockSpec((1, 1, D), lambda b, i: (b, 0, 0))
    return pl.pallas_call(
        _in_proj_kernel,
        out_shape=(q_shape, q_shape, kraw_shape, kraw_shape, k_shape, vt_shape, k_shape, vt_shape,
                   k_shape, vt_shape,
                   jax.ShapeDtypeStruct((B, S, WIDTH), bf), jax.ShapeDtypeStruct((B, S, WIDTH), bf),
                   jax.ShapeDtypeStruct((B, S, D), bf), jax.ShapeDtypeStruct((B, S, D), bf),
                   jax.ShapeDtypeStruct((B, GATE_ROWS, S), jnp.float32)),
        grid=(B, S // tm),
        in_specs=[row_spec(D), vec_spec, vec_spec,
                  pl.BlockSpec((1, D), lambda b, i: (0, 0)),
                  pl.BlockSpec((D, _C_END), lambda b, i: (0, 0), pipeline_mode=pl.Buffered(1))],
        out_specs=(q_spec, q_spec, kraw_spec, kraw_spec, k_spec, vt_spec, k_spec, vt_spec,
                   k_spec, vt_spec,
                   row_spec(WIDTH), row_spec(WIDTH), row_spec(D), row_spec(D),
                   pl.BlockSpec((1, GATE_ROWS, tm), lambda b, i: (b, 0, i))),
        compiler_params=pltpu.CompilerParams(
            dimension_semantics=("parallel", "parallel"), vmem_limit_bytes=VMEM_LIMIT),
        name="in_proj",
    )(x, shift, scale, gpre, w)


def _compress_kernel(t_ref, pe_ref, w1_ref, w2_ref, o_ref, ot_ref):
    half = CMP_STRIDE * HEAD_DIM
    t = t_ref[0, 0]
    w1 = w1_ref[0]
    w1b = w1.astype(jnp.bfloat16)
    first = _dot(t, w1b[:half])
    second = _dot(t, w1b[half:])
    n = second.shape[0]
    second = pltpu.roll(second, n - 1, 0)
    pe = jnp.broadcast_to(pe_ref[0], (8, 2 * half))
    bias = jnp.dot(pe, w1, preferred_element_type=jnp.float32,
                   precision=lax.Precision.HIGHEST)[0:1]
    pre = first + second + bias
    hid = pre * _sigmoid(pre)
    out = _dot(hid.astype(jnp.bfloat16), w2_ref[0].astype(jnp.bfloat16))
    lane = _iota((n, FEAT), 1)
    c = _iota((n, FEAT), 0)
    end = c * CMP_STRIDE + (CMP_BLOCK - 1)
    feat = jnp.where(lane == LANE_BLK, lax.shift_right_logical(end, 6).astype(jnp.float32),
                     jnp.where(lane == LANE_POS, (end & (SLC_BLOCK - 1)).astype(jnp.float32), 0.0))
    o_ref[0, 0] = jnp.where(lane < HEAD_DIM, out, feat).astype(jnp.bfloat16)
    ot_ref[0, 0] = out.T[:HEAD_DIM].astype(jnp.bfloat16)


def _compress(t, pe, w1, w2):
    _, BG, n, wdt = t.shape
    return pl.pallas_call(
        _compress_kernel,
        out_shape=(jax.ShapeDtypeStruct((2, BG, n, FEAT), jnp.bfloat16),
                   jax.ShapeDtypeStruct((2, BG, HEAD_DIM, n), jnp.bfloat16)),
        grid=(2, BG),
        in_specs=[pl.BlockSpec((1, 1, n, wdt), lambda a, i: (a, i, 0, 0)),
                  pl.BlockSpec((1, 1, 2 * wdt), lambda a, i: (a, 0, 0)),
                  pl.BlockSpec((1, 2 * wdt, CMP_HIDDEN), lambda a, i: (a, 0, 0)),
                  pl.BlockSpec((1, CMP_HIDDEN, FEAT), lambda a, i: (a, 0, 0))],
        out_specs=(pl.BlockSpec((1, 1, n, FEAT), lambda a, i: (a, i, 0, 0)),
                   pl.BlockSpec((1, 1, HEAD_DIM, n), lambda a, i: (a, i, 0, 0))),
        compiler_params=pltpu.CompilerParams(dimension_semantics=("parallel", "parallel")),
        name="compress",
    )(t, pe, w1, w2)


def _head_slope(g, r):
    return jnp.where(g == 0, 1.0, 2.0 ** -GROUP).astype(jnp.float32) * (2.0 ** -(r + 1))


def _per_head(x):
    return jnp.concatenate([x] * GROUP, axis=1)


def _band_scores(q, k_ref, q0, window):
    return _dot_nt(k_ref[0, 0, pl.ds(pl.multiple_of(q0, TQ), window + TQ), :], q)


def _band_finish(st, vt_ref, q0, window, sink_row):
    n_keys = window + TQ
    start = pl.multiple_of(q0, TQ)
    row, col = _iota((SUB, TQ), 0), _iota((SUB, TQ), 1)
    too_old = _per_head(jnp.where(row > col, 0.0, NEG))
    future = _per_head(jnp.where(row <= col, 0.0, NEG))
    middle = [st[SUB:n_keys - SUB]] if n_keys > 2 * SUB else []
    s = jnp.concatenate([st[:SUB] + too_old] + middle + [st[n_keys - SUB:] + future], axis=0)
    m = jnp.max(s, axis=0, keepdims=True)
    if sink_row is not None:
        m = jnp.maximum(m, sink_row)
    p = jnp.exp(s - m)
    l = jnp.sum(p, axis=0, keepdims=True)
    if sink_row is not None:
        l = l + jnp.exp(sink_row - m)
    o = _dot(vt_ref[0, 0, :, pl.ds(start, n_keys)], p.astype(jnp.bfloat16))
    return o / l


def _store_heads(o_t, sz_ref, o_ref):
    stacked = jnp.concatenate([o_t[:, r * TQ:(r + 1) * TQ] for r in range(GROUP)], axis=0)
    o_ref[0] = (stacked.T * sz_ref[0].astype(jnp.float32)).astype(jnp.bfloat16)


def _importance_matrix(n_cmp):
    ratio = SLC_BLOCK // CMP_STRIDE
    span = CMP_BLOCK // CMP_STRIDE
    j = _iota((SLC_BLOCK, n_cmp), 0)
    c = _iota((SLC_BLOCK, n_cmp), 1)
    off = ratio * j - c
    w = jnp.zeros((SLC_BLOCK, n_cmp), jnp.float32)
    for mm in range(ratio):
        for nn in range(span):
            w = w + jnp.where(off == mm + nn, 1.0, 0.0)
    w = jnp.where(c < n_cmp - 1, w, 0.0)
    return w.astype(jnp.bfloat16)


def _selection_rank(score):
    nsb = score.shape[0]
    groups = [score[8 * a:8 * a + 8] for a in range(nsb // 8)]
    sub = _iota((8, TQ), 0)
    rank = [jnp.zeros((8, TQ), jnp.float32) for _ in groups]
    for i in range(nsb):
        row = jnp.broadcast_to(score[i:i + 1, :], (8, TQ))
        for a, grp in enumerate(groups):
            if a < i // 8:
                ahead = jnp.where(row > grp, 1.0, 0.0)
            elif a > i // 8:
                ahead = jnp.where(row >= grp, 1.0, 0.0)
            else:
                ahead = jnp.where(sub > i % 8, jnp.where(row >= grp, 1.0, 0.0),
                                  jnp.where(row > grp, 1.0, 0.0))
            rank[a] = rank[a] + ahead
    return jnp.concatenate(rank, axis=0)


def _mixers_kernel(sinks_ref, qa_ref, qb_ref, kc_ref, vct_ref, ks_ref, vst_ref, kw_ref, vwt_ref,
                   kb_ref, vbt_ref, onehot_ref, gt_ref, sza_ref, szb_ref, oa_ref, ob_ref, s_buf):
    g = pl.program_id(1)
    qi = pl.program_id(2)
    q0 = qi * TQ
    q = qa_ref[0, 0].reshape(GROUP * TQ, FEAT)

    n_cmp = kc_ref.shape[1]
    st_cmp = _dot_nt(kc_ref[0], q)
    st_swa = _band_scores(qb_ref[0, 0].reshape(GROUP * TQ, FEAT), kb_ref, q0, SWA_WINDOW)
    st_win = _band_scores(q, kw_ref, q0, NSA_WINDOW)

    qpos_c = q0 + _iota((n_cmp, TQ), 1)
    end_c = _iota((n_cmp, TQ), 0) * CMP_STRIDE + (CMP_BLOCK - 1)
    hidden = jnp.where(qpos_c >= end_c, 0.0, NEG)
    any_valid = (qpos_c[0:1] >= CMP_BLOCK - 1).astype(jnp.float32)
    p_sum = jnp.zeros((n_cmp, TQ), jnp.float32)
    o_cmp = []
    for r in range(GROUP):
        sr = st_cmp[:, r * TQ:(r + 1) * TQ] + hidden
        er = jnp.exp(sr - jnp.max(sr, axis=0, keepdims=True))
        inv = any_valid / jnp.sum(er, axis=0, keepdims=True)
        p_sum = p_sum + er * inv
        o_cmp.append(_dot(vct_ref[0], er.astype(jnp.bfloat16)) * inv)
    o_cmp = jnp.concatenate(o_cmp, axis=1)

    wimp = _importance_matrix(n_cmp)
    p_hi = p_sum.astype(jnp.bfloat16)
    rem = p_sum - p_hi.astype(jnp.float32)
    p_mid = rem.astype(jnp.bfloat16)
    p_lo = (rem - p_mid.astype(jnp.float32)).astype(jnp.bfloat16)
    imp = _dot(wimp, p_hi) + _dot(wimp, p_mid) + _dot(wimp, p_lo)

    nsb = imp.shape[0]
    jrow = _iota((nsb, TQ), 0)
    cur = lax.shift_right_logical(q0 + _iota((nsb, TQ), 1), 6)
    causal = jrow <= cur
    forced = (jrow == 0) | (jrow == cur) | (jrow == cur - 1)
    score = jnp.where(causal, jnp.where(forced, BIG, imp), NEG)
    rank = _selection_rank(score)
    unselected = jnp.where(causal, jnp.where(rank < float(N_SELECT), 0.0, NEG_FEATURE), NEG_FEATURE)
    mask_feat = jnp.concatenate([unselected, jnp.zeros((FEAT - nsb, TQ), jnp.float32)], axis=0).T
    q_sel = jnp.concatenate(
        [q, jnp.concatenate([mask_feat.astype(jnp.bfloat16)] * GROUP, axis=0)], axis=1)

    tk = TK_SLC
    diag_pair = (q0 + TQ - 1) // (2 * tk)

    def slc_scores(j):
        k0 = pl.multiple_of(j * tk, tk)
        k_aug = jnp.concatenate([ks_ref[0, 0, pl.ds(k0, tk), :], onehot_ref[pl.ds(k0, tk), :]], axis=1)
        return _dot_nt(k_aug, q_sel)

    def slc_update(j, slot, carry, near_diagonal):
        m, l, acc = carry
        k0 = pl.multiple_of(j * tk, tk)
        s = s_buf[slot]
        if near_diagonal:
            dist = (q0 + _iota((tk, TQ), 1)) - (k0 + _iota((tk, TQ), 0))
            s = s + _per_head(jnp.where(dist >= 0, 0.0, NEG))
        m_new = jnp.maximum(m, jnp.max(s, axis=0, keepdims=True))
        alpha = jnp.exp(m - m_new)
        p = jnp.exp(s - m_new)
        l = alpha * l + jnp.sum(p, axis=0, keepdims=True)
        acc = alpha * acc + _dot(vst_ref[0, 0, :, pl.ds(k0, tk)], p.astype(jnp.bfloat16))
        return m_new, l, acc

    s_buf[2] = slc_scores(2 * diag_pair)
    s_buf[3] = slc_scores(2 * diag_pair + 1)
    s_buf[0] = slc_scores(0)

    qpos = (q0 + _iota((1, TQ), 1)).astype(jnp.float32)
    sink_row = jnp.concatenate(
        [sinks_ref[g * GROUP + r] + _head_slope(g, r) * qpos for r in range(GROUP)], axis=1)
    _store_heads(_band_finish(st_swa, vbt_ref, q0, SWA_WINDOW, sink_row), szb_ref, ob_ref)

    o_win = _band_finish(st_win, vwt_ref, q0, NSA_WINDOW, None)

    n = GROUP * TQ
    carry = (jnp.full((1, n), M_FLOOR, jnp.float32), jnp.zeros((1, n), jnp.float32),
             jnp.zeros((HEAD_DIM, n), jnp.float32))
    carry = slc_update(2 * diag_pair, 2, carry, True)
    carry = slc_update(2 * diag_pair + 1, 3, carry, True)

    def slc_pair(jj, carry):
        j = 2 * jj
        s_buf[1] = slc_scores(j + 1)
        carry = slc_update(j, 0, carry, False)
        s_buf[0] = slc_scores(j + 2)
        return slc_update(j + 1, 1, carry, False)

    _, l, acc = lax.fori_loop(0, diag_pair, slc_pair, carry)
    o_slc = acc / l

    pieces = []
    for r in range(GROUP):
        sl = slice(r * TQ, (r + 1) * TQ)
        gate = [gt_ref[0, pl.ds(br * N_HEADS + g * GROUP + r, 1), :] for br in range(3)]
        pieces.append(gate[0] * o_cmp[:, sl] + gate[1] * o_slc[:, sl] + gate[2] * o_win[:, sl])
    _store_heads(jnp.concatenate(pieces, axis=1), sza_ref, oa_ref)


def _mixers(sinks, qa, qb, kc, vct, ks, vst, kw, vwt, kb, vbt, onehot, gt, sza, szb):
    B, _, _, S, _ = qa.shape
    n_cmp = kc.shape[1]
    q_spec = pl.BlockSpec((1, 1, GROUP, TQ, FEAT), lambda b, g, i: (b, g, 0, i, 0))
    seq_k = lambda n: pl.BlockSpec((1, 1, n, FEAT), lambda b, g, i: (b, g, 0, 0))
    seq_vt = lambda n: pl.BlockSpec((1, 1, HEAD_DIM, n), lambda b, g, i: (b, g, 0, 0))
    o_spec = pl.BlockSpec((1, TQ, GROUP * HEAD_DIM), lambda b, g, i: (b, i, g))
    o_shape = jax.ShapeDtypeStruct((B, S, WIDTH), jnp.bfloat16)
    return pl.pallas_call(
        _mixers_kernel,
        out_shape=(o_shape, o_shape),
        grid=(B, N_KV, S // TQ),
        in_specs=[pl.BlockSpec(memory_space=pltpu.SMEM), q_spec, q_spec,
                  pl.BlockSpec((1, n_cmp, FEAT), lambda b, g, i: (b * N_KV + g, 0, 0)),
                  pl.BlockSpec((1, HEAD_DIM, n_cmp), lambda b, g, i: (b * N_KV + g, 0, 0)),
                  seq_k(S), seq_vt(S), seq_k(kw.shape[2]), seq_vt(kw.shape[2]),
                  seq_k(kb.shape[2]), seq_vt(kb.shape[2]),
                  pl.BlockSpec((S, FEAT), lambda b, g, i: (0, 0)),
                  pl.BlockSpec((1, GATE_ROWS, TQ), lambda b, g, i: (b, 0, i)),
                  o_spec, o_spec],
        out_specs=(o_spec, o_spec),
        scratch_shapes=[pltpu.VMEM((4, TK_SLC, GROUP * TQ), jnp.float32)],
        compiler_params=pltpu.CompilerParams(
            dimension_semantics=("parallel", "parallel", "arbitrary"),
            vmem_limit_bytes=VMEM_LIMIT),
        name="mixers",
    )(sinks, qa, qb, kc, vct, ks, vst, kw, vwt, kb, vbt, onehot, gt, sza, szb)


def _out_proj_kernel(x_ref, oza_ref, ozb_ref, sma_ref, smb_ref, gate_ref, gpost_ref,
                     woa_ref, wob_ref, wout_ref, o_ref):
    ya = _dot(oza_ref[0], woa_ref[...])
    yb = _dot(ozb_ref[0], wob_ref[...])
    y = sma_ref[0].astype(jnp.float32) * ya + smb_ref[0].astype(jnp.float32) * yb
    yo = _dot(y.astype(jnp.bfloat16), wout_ref[...])
    ms = jnp.mean(yo * yo, axis=-1, keepdims=True)
    normed = yo * lax.rsqrt(ms + RMS_EPS) * gpost_ref[...]
    o_ref[0] = x_ref[0] + gate_ref[0] * normed


def _out_proj(x, oza, ozb, sma, smb, gate, gpost, woa, wob, wout):
    B, S, D = x.shape
    tm = TM_PROJ
    row_spec = lambda n: pl.BlockSpec((1, tm, n), lambda b, i: (b, i, 0))
    const = lambda shape: pl.BlockSpec(shape, lambda b, i: (0,) * len(shape))
    return pl.pallas_call(
        _out_proj_kernel,
        out_shape=jax.ShapeDtypeStruct((B, S, D), jnp.float32),
        grid=(B, S // tm),
        in_specs=[row_spec(D), row_spec(WIDTH), row_spec(WIDTH), row_spec(D), row_spec(D),
                  pl.BlockSpec((1, 1, D), lambda b, i: (b, 0, 0)), const((1, D)),
                  const((WIDTH, D)), const((WIDTH, D)), const((D, D))],
        out_specs=row_spec(D),
        compiler_params=pltpu.CompilerParams(
            dimension_semantics=("parallel", "parallel"), vmem_limit_bytes=VMEM_LIMIT),
        name="out_proj",
    )(x, oza, ozb, sma, smb, gate, gpost, woa, wob, wout)


def _arrange_w_in(w):
    o = 0
    seg = {}
    for name, n in (("qa", WIDTH), ("kvc", 2 * KVW), ("kvs", 2 * KVW), ("kvw", 2 * KVW),
                    ("g", N_GATES), ("za", WIDTH), ("qb", WIDTH), ("kvb", 2 * KVW),
                    ("zb", WIDTH), ("m", 2 * D_MODEL)):
        seg[name] = w[:, o:o + n]
        o += n
    scale = HEAD_DIM ** -0.5
    gpad = jnp.pad(seg["g"], ((0, 0), (0, GATE_PAD - N_GATES)))
    cols = [seg["qa"] * scale, seg["qb"] * scale, seg["kvc"], seg["kvs"], seg["kvw"], seg["kvb"],
            seg["za"], seg["zb"], seg["m"], gpad]
    return jnp.concatenate(cols, axis=1).astype(jnp.bfloat16)


def _front_pad(k, vt, n):
    B, G = k.shape[:2]
    pad_row = jnp.zeros((FEAT,), jnp.float32).at[LANE_BLK].set(PAD_KEY_BLOCK).astype(k.dtype)
    k_pad = jnp.broadcast_to(pad_row, (B, G, n, FEAT))
    vt_pad = jnp.zeros((B, G, HEAD_DIM, n), vt.dtype)
    return jnp.concatenate([k_pad, k], axis=2), jnp.concatenate([vt_pad, vt], axis=3)


def _block_onehot(S):
    blk = jnp.arange(S, dtype=jnp.int32)[:, None] // SLC_BLOCK
    return (blk == jnp.arange(FEAT, dtype=jnp.int32)[None, :]).astype(jnp.bfloat16)


def kernel(x, c, w_ada, b_ada, g_pre, g_post, w_in, pe_cmp_k, pe_cmp_v, w_cmp_k1, w_cmp_k2,
           w_cmp_v1, w_cmp_v2, w_o_nsa, w_o_swa, w_out, sinks):
    B, S, D = x.shape
    depth = w_in.shape[0]
    bf = jnp.bfloat16
    for l in range(depth):
        c8 = jnp.pad(c, ((0, 8 - B), (0, 0)))
        mod = _adaln_mod(c8, w_ada[l], b_ada[l][None, :])[:B]
        shift, scale, gate = (mod[:, None, i * D:(i + 1) * D] for i in range(3))

        (qa, qb, kc_raw, vc_raw, ks, vst, kw, vwt, kb, vbt, sza, szb, sma, smb, gt) = _in_proj(
            x, shift, scale, g_pre[l][None, :], _arrange_w_in(w_in[l]))

        rows = S // CMP_STRIDE
        t = jnp.stack([kc_raw, vc_raw]).reshape(2, B * N_KV, rows, CMP_STRIDE * HEAD_DIM)
        pe = jnp.stack([pe_cmp_k[l], pe_cmp_v[l]]).reshape(2, 1, CMP_BLOCK * HEAD_DIM)
        w2 = jnp.pad(jnp.stack([w_cmp_k2[l], w_cmp_v2[l]]), ((0, 0), (0, 0), (0, FEAT - HEAD_DIM)))
        cmp_o, cmp_ot = _compress(t, pe, jnp.stack([w_cmp_k1[l], w_cmp_v1[l]]), w2)

        kw, vwt = _front_pad(kw, vwt, NSA_WINDOW)
        kb, vbt = _front_pad(kb, vbt, SWA_WINDOW)
        oza, ozb = _mixers(sinks[l], qa, qb, cmp_o[0], cmp_ot[1], ks, vst, kw, vwt, kb, vbt,
                           _block_onehot(S), gt, sza, szb)

        x = _out_proj(x, oza, ozb, sma, smb, gate, g_post[l][None, :],
                      w_o_nsa[l].astype(bf), w_o_swa[l].astype(bf), w_out[l].astype(bf))
    return x
```

```python
import jax
import jax.numpy as jnp
import numpy as np
from jax import lax
from jax.experimental import pallas as pl
from jax.experimental.pallas import tpu as pltpu

D_MODEL = 1024
HEAD_DIM = 64
N_HEADS = 8
N_KV = 2
GROUP = N_HEADS // N_KV
CMP_BLOCK = 32
CMP_STRIDE = 16
CMP_HIDDEN = 256
SLC_BLOCK = 64
N_SELECT = 16
NSA_WINDOW = 512
SWA_WINDOW = 128
RMS_EPS = 1e-6
NEG = -1e30
BIG = 1e30
M_FLOOR = -1e20
NEG_FEATURE = -(2.0 ** 100)
PAD_KEY_BLOCK = -(2.0 ** 90)

WIDTH = N_HEADS * HEAD_DIM
KVW = N_KV * HEAD_DIM
FEAT = 128
V_ROWS = 80


def _bf16_pieces(x, n):
    out = []
    for _ in range(n):
        bits = np.array(x, np.float32).view(np.uint32)
        bits = (bits + np.uint32(0x7FFF) + ((bits >> np.uint32(16)) & np.uint32(1))) & np.uint32(0xFFFF0000)
        piece = float(bits.view(np.float32))
        out.append(piece)
        x -= piece
    return tuple(out)


LOG2E_PIECES = _bf16_pieces(1.4426950408889634, 3)
LOG2E = sum(LOG2E_PIECES)
N_PIECES = len(LOG2E_PIECES)
LANE_BLK = HEAD_DIM
LANE_POS = HEAD_DIM + N_PIECES
N_GATES = 3 * N_HEADS
GATE_PAD = 128
GATE_ROWS = 32

VMEM_LIMIT = 56 * 1024 * 1024

TM_PROJ = 512
TQ = 256
TK_SLC = 512
SUB = 128

_NT = (((1,), (1,)), ((), ()))


def _dot(a, b):
    return jnp.dot(a, b, preferred_element_type=jnp.float32)


def _dot_nt(a, b):
    return lax.dot_general(a, b, _NT, preferred_element_type=jnp.float32)


def _sigmoid(v):
    return 1.0 / (1.0 + jnp.exp(-v))


def _iota(shape, dim):
    return lax.broadcasted_iota(jnp.int32, shape, dim)


def _key_features(lane, pos):
    blk = lax.shift_right_logical(pos, 6).astype(jnp.float32)
    within = (pos & (SLC_BLOCK - 1)).astype(jnp.float32)
    return jnp.where(lane < LANE_POS, blk, jnp.where(lane < LANE_POS + N_PIECES, within, 0.0))


def _query_features(lane, slope):
    feat = jnp.zeros(lane.shape, jnp.float32)
    for i, piece in enumerate(LOG2E_PIECES):
        feat = jnp.where(lane == LANE_BLK + i, slope * SLC_BLOCK * piece, feat)
        feat = jnp.where(lane == LANE_POS + i, slope * piece, feat)
    return feat


def _adaln_kernel(c_ref, w_ref, b_ref, o_ref):
    o_ref[...] = jnp.dot(c_ref[...], w_ref[...], preferred_element_type=jnp.float32,
                         precision=lax.Precision.HIGHEST) + b_ref[...]


def _adaln_mod(c8, w, b):
    n = w.shape[1]
    bn = 1024
    return pl.pallas_call(
        _adaln_kernel,
        out_shape=jax.ShapeDtypeStruct((c8.shape[0], n), jnp.float32),
        grid=(n // bn,),
        in_specs=[pl.BlockSpec((c8.shape[0], D_MODEL), lambda j: (0, 0)),
                  pl.BlockSpec((D_MODEL, bn), lambda j: (0, j)),
                  pl.BlockSpec((1, bn), lambda j: (0, j))],
        out_specs=pl.BlockSpec((c8.shape[0], bn), lambda j: (0, j)),
        name="adaln_mod",
    )(c8, w, b)


_C_Q = 0
_C_KV = 2 * WIDTH
_C_Z = _C_KV + 8 * KVW
_C_M = _C_Z + 2 * WIDTH
_C_G = _C_M + 2 * D_MODEL
_C_END = _C_G + GATE_PAD


def _with_features(pair, odd, feat, lane):
    src = pltpu.roll(pair, HEAD_DIM, 1) if odd else pair
    return jnp.where(lane < HEAD_DIM, src, feat).astype(jnp.bfloat16)


def _in_proj_kernel(x_ref, shift_ref, scale_ref, gpre_ref, w_ref,
                    qa_ref, qb_ref, kc_ref, vc_ref, ks_ref, vst_ref, kw_ref, vwt_ref,
                    kb_ref, vbt_ref, sza_ref, szb_ref, sma_ref, smb_ref, gt_ref):
    tm = x_ref.shape[1]
    xf = x_ref[0]
    ms = jnp.mean(xf * xf, axis=-1, keepdims=True)
    y = xf * lax.rsqrt(ms + RMS_EPS) * gpre_ref[...]
    h = y * (1.0 + scale_ref[0]) + shift_ref[0]
    hb = h.astype(jnp.bfloat16)

    lane = _iota((tm, FEAT), 1)
    pos = pl.program_id(1) * tm + _iota((tm, FEAT), 0)
    key_feat = _key_features(lane, pos)

    acc = _dot(hb, w_ref[:, _C_Q:_C_KV])
    for head in range(N_HEADS):
        q_feat = _query_features(lane, 2.0 ** -(head + 1))
        g, r, odd = head // GROUP, head % GROUP, head % 2
        c = (head // 2) * FEAT
        qa_ref[0, g, r] = _with_features(acc[:, c:c + FEAT], odd, q_feat, lane)
        qb_ref[0, g, r] = _with_features(acc[:, WIDTH + c:WIDTH + c + FEAT], odd, q_feat, lane)

    acc = _dot(hb, w_ref[:, _C_KV:_C_Z])
    for g in range(N_KV):
        c = g * HEAD_DIM
        kc_ref[0, g] = acc[:, c:c + HEAD_DIM].astype(jnp.bfloat16)
        vc_ref[0, g] = acc[:, KVW + c:KVW + c + HEAD_DIM].astype(jnp.bfloat16)
        ks_ref[0, g] = _with_features(acc[:, 2 * KVW:3 * KVW], g, key_feat, lane)
        kw_ref[0, g] = _with_features(acc[:, 4 * KVW:5 * KVW], g, key_feat, lane)
        kb_ref[0, g] = _with_features(acc[:, 6 * KVW:7 * KVW], g, key_feat, lane)
    vst = acc[:, 3 * KVW:4 * KVW].T
    vwt = acc[:, 5 * KVW:6 * KVW].T
    vbt = acc[:, 7 * KVW:8 * KVW].T
    ones_rows = jnp.where(_iota((V_ROWS - HEAD_DIM, tm), 0) == 0, 1.0, 0.0).astype(jnp.bfloat16)
    for g in range(N_KV):
        c = g * HEAD_DIM
        for ref, vt in ((vst_ref, vst), (vwt_ref, vwt), (vbt_ref, vbt)):
            ref[0, g] = jnp.concatenate([vt[c:c + HEAD_DIM].astype(jnp.bfloat16), ones_rows], axis=0)

    acc = _dot(hb, w_ref[:, _C_Z:_C_M])
    sz = acc * _sigmoid(acc)
    sza_ref[0] = sz[:, :WIDTH].astype(jnp.bfloat16)
    szb_ref[0] = sz[:, WIDTH:].astype(jnp.bfloat16)

    acc = _dot(hb, w_ref[:, _C_M:_C_M + D_MODEL])
    sma_ref[0] = _sigmoid(acc).astype(jnp.bfloat16)
    acc = _dot(hb, w_ref[:, _C_M + D_MODEL:_C_G])
    smb_ref[0] = _sigmoid(acc).astype(jnp.bfloat16)

    acc = _dot(hb, w_ref[:, _C_G:_C_END])
    gt_ref[0] = _sigmoid(acc).T[:GATE_ROWS]


def _in_proj(x, shift, scale, gpre, w):
    B, S, D = x.shape
    tm = TM_PROJ
    bf = jnp.bfloat16
    q_shape = jax.ShapeDtypeStruct((B, N_KV, GROUP, S, FEAT), bf)
    kraw_shape = jax.ShapeDtypeStruct((B, N_KV, S, HEAD_DIM), bf)
    k_shape = jax.ShapeDtypeStruct((B, N_KV, S, FEAT), bf)
    vt_shape = jax.ShapeDtypeStruct((B, N_KV, V_ROWS, S), bf)
    q_spec = pl.BlockSpec((1, N_KV, GROUP, tm, FEAT), lambda b, i: (b, 0, 0, i, 0))
    kraw_spec = pl.BlockSpec((1, N_KV, tm, HEAD_DIM), lambda b, i: (b, 0, i, 0))
    k_spec = pl.BlockSpec((1, N_KV, tm, FEAT), lambda b, i: (b, 0, i, 0))
    vt_spec = pl.BlockSpec((1, N_KV, V_ROWS, tm), lambda b, i: (b, 0, 0, i))
    row_spec = lambda n: pl.BlockSpec((1, tm, n), lambda b, i: (b, i, 0))
    vec_spec = pl.BlockSpec((1, 1, D), lambda b, i: (b, 0, 0))
    return pl.pallas_call(
        _in_proj_kernel,
        out_shape=(q_shape, q_shape, kraw_shape, kraw_shape, k_shape, vt_shape, k_shape, vt_shape,
                   k_shape, vt_shape,
                   jax.ShapeDtypeStruct((B, S, WIDTH), bf), jax.ShapeDtypeStruct((B, S, WIDTH), bf),
                   jax.ShapeDtypeStruct((B, S, D), bf), jax.ShapeDtypeStruct((B, S, D), bf),
                   jax.ShapeDtypeStruct((B, GATE_ROWS, S), jnp.float32)),
        grid=(B, S // tm),
        in_specs=[row_spec(D), vec_spec, vec_spec,
                  pl.BlockSpec((1, D), lambda b, i: (0, 0)),
                  pl.BlockSpec((D, _C_END), lambda b, i: (0, 0), pipeline_mode=pl.Buffered(1))],
        out_specs=(q_spec, q_spec, kraw_spec, kraw_spec, k_spec, vt_spec, k_spec, vt_spec,
                   k_spec, vt_spec,
                   row_spec(WIDTH), row_spec(WIDTH), row_spec(D), row_spec(D),
                   pl.BlockSpec((1, GATE_ROWS, tm), lambda b, i: (b, 0, i))),
        compiler_params=pltpu.CompilerParams(
            dimension_semantics=("parallel", "parallel"), vmem_limit_bytes=VMEM_LIMIT),
        name="in_proj",
    )(x, shift, scale, gpre, w)


def _compress_kernel(t_ref, pe_ref, w1_ref, w2_ref, o_ref, ot_ref):
    half = CMP_STRIDE * HEAD_DIM
    t = t_ref[0, 0]
    w1 = w1_ref[0]
    w1b = w1.astype(jnp.bfloat16)
    first = _dot(t, w1b[:half])
    second = _dot(t, w1b[half:])
    n = second.shape[0]
    second = pltpu.roll(second, n - 1, 0)
    pe = jnp.broadcast_to(pe_ref[0], (8, 2 * half))
    bias = jnp.dot(pe, w1, preferred_element_type=jnp.float32,
                   precision=lax.Precision.HIGHEST)[0:1]
    pre = first + second + bias
    hid = pre * _sigmoid(pre)
    out = _dot(hid.astype(jnp.bfloat16), w2_ref[0].astype(jnp.bfloat16))
    lane = _iota((n, FEAT), 1)
    c = _iota((n, FEAT), 0)
    end = c * CMP_STRIDE + (CMP_BLOCK - 1)
    o_ref[0, 0] = jnp.where(lane < HEAD_DIM, out, _key_features(lane, end)).astype(jnp.bfloat16)
    ot_ref[0, 0] = out.T[:HEAD_DIM].astype(jnp.bfloat16)


def _compress(t, pe, w1, w2):
    _, BG, n, wdt = t.shape
    return pl.pallas_call(
        _compress_kernel,
        out_shape=(jax.ShapeDtypeStruct((2, BG, n, FEAT), jnp.bfloat16),
                   jax.ShapeDtypeStruct((2, BG, HEAD_DIM, n), jnp.bfloat16)),
        grid=(2, BG),
        in_specs=[pl.BlockSpec((1, 1, n, wdt), lambda a, i: (a, i, 0, 0)),
                  pl.BlockSpec((1, 1, 2 * wdt), lambda a, i: (a, 0, 0)),
                  pl.BlockSpec((1, 2 * wdt, CMP_HIDDEN), lambda a, i: (a, 0, 0)),
                  pl.BlockSpec((1, CMP_HIDDEN, FEAT), lambda a, i: (a, 0, 0))],
        out_specs=(pl.BlockSpec((1, 1, n, FEAT), lambda a, i: (a, i, 0, 0)),
                   pl.BlockSpec((1, 1, HEAD_DIM, n), lambda a, i: (a, i, 0, 0))),
        compiler_params=pltpu.CompilerParams(dimension_semantics=("parallel", "parallel")),
        name="compress",
    )(t, pe, w1, w2)


def _head_slope(g, r):
    return jnp.where(g == 0, 1.0, 2.0 ** -GROUP).astype(jnp.float32) * (2.0 ** -(r + 1))


def _per_head(x):
    return jnp.concatenate([x] * GROUP, axis=1)


def _band_scores(q, k_ref, q0, window):
    return _dot_nt(k_ref[0, 0, pl.ds(pl.multiple_of(q0, TQ), window + TQ), :], q)


def _band_finish(st, vt_ref, q0, window, sink_row):
    n_keys = window + TQ
    start = pl.multiple_of(q0, TQ)
    parts = []
    for r0 in range(0, n_keys, SUB):
        part = st[r0:r0 + SUB]
        row, col = r0 + _iota((SUB, TQ), 0), _iota((SUB, TQ), 1)
        if r0 < TQ:
            part = part + _per_head(jnp.where(row > col, 0.0, NEG))
        if r0 + SUB - 1 > window:
            part = part + _per_head(jnp.where(row <= col + window, 0.0, NEG))
        parts.append(part)
    s = jnp.concatenate(parts, axis=0)
    m = jnp.max(s, axis=0, keepdims=True)
    if sink_row is not None:
        m = jnp.maximum(m, sink_row)
    p = jnp.exp2(s - m)
    o = _dot(vt_ref[0, 0, :, pl.ds(start, n_keys)], p.astype(jnp.bfloat16))
    l = o[HEAD_DIM:HEAD_DIM + 1]
    if sink_row is not None:
        l = l + jnp.exp2(sink_row - m)
    return o[:HEAD_DIM] / l


def _store_heads(o_t, sz_ref, o_ref):
    stacked = jnp.concatenate([o_t[:, r * TQ:(r + 1) * TQ] for r in range(GROUP)], axis=0)
    o_ref[0] = (stacked.T * sz_ref[0].astype(jnp.float32)).astype(jnp.bfloat16)


def _importance_matrix(n_cmp):
    ratio = SLC_BLOCK // CMP_STRIDE
    span = CMP_BLOCK // CMP_STRIDE
    j = _iota((SLC_BLOCK, n_cmp), 0)
    c = _iota((SLC_BLOCK, n_cmp), 1)
    off = ratio * j - c
    w = jnp.zeros((SLC_BLOCK, n_cmp), jnp.float32)
    for mm in range(ratio):
        for nn in range(span):
            w = w + jnp.where(off == mm + nn, 1.0, 0.0)
    w = jnp.where(c < n_cmp - 1, w, 0.0)
    return w.astype(jnp.bfloat16)


def _selection_rank(score):
    nsb = score.shape[0]
    groups = [score[8 * a:8 * a + 8] for a in range(nsb // 8)]
    sub = _iota((8, TQ), 0)
    rank = [jnp.zeros((8, TQ), jnp.float32) for _ in groups]
    for i in range(nsb):
        row = jnp.broadcast_to(score[i:i + 1, :], (8, TQ))
        for a, grp in enumerate(groups):
            if a < i // 8:
                ahead = jnp.where(row > grp, 1.0, 0.0)
            elif a > i // 8:
                ahead = jnp.where(row >= grp, 1.0, 0.0)
            else:
                ahead = jnp.where(sub > i % 8, jnp.where(row >= grp, 1.0, 0.0),
                                  jnp.where(row > grp, 1.0, 0.0))
            rank[a] = rank[a] + ahead
    return jnp.concatenate(rank, axis=0)


def _mixers_kernel(sinks_ref, qa_ref, qb_ref, kc_ref, vct_ref, ks_ref, vst_ref, kw_ref, vwt_ref,
                   kb_ref, vbt_ref, onehot_ref, gt_ref, sza_ref, szb_ref, oa_ref, ob_ref, s_buf):
    g = pl.program_id(1)
    qi = pl.program_id(2)
    q0 = qi * TQ
    q = qa_ref[0, 0].reshape(GROUP * TQ, FEAT)

    n_cmp = kc_ref.shape[1]
    st_cmp = _dot_nt(kc_ref[0], q)
    st_swa = _band_scores(qb_ref[0, 0].reshape(GROUP * TQ, FEAT), kb_ref, q0, SWA_WINDOW)
    st_win = _band_scores(q, kw_ref, q0, NSA_WINDOW)

    qpos_c = q0 + _iota((n_cmp, TQ), 1)
    end_c = _iota((n_cmp, TQ), 0) * CMP_STRIDE + (CMP_BLOCK - 1)
    hidden = jnp.where(qpos_c >= end_c, 0.0, NEG)
    any_valid = (qpos_c[0:1] >= CMP_BLOCK - 1).astype(jnp.float32)
    p_sum = jnp.zeros((n_cmp, TQ), jnp.float32)
    o_cmp = []
    for r in range(GROUP):
        sr = st_cmp[:, r * TQ:(r + 1) * TQ] + hidden
        er = jnp.exp2(sr - jnp.max(sr, axis=0, keepdims=True))
        inv = any_valid / jnp.sum(er, axis=0, keepdims=True)
        p_sum = p_sum + er * inv
        o_cmp.append(_dot(vct_ref[0], er.astype(jnp.bfloat16)) * inv)
    o_cmp = jnp.concatenate(o_cmp, axis=1)

    wimp = _importance_matrix(n_cmp)
    p_hi = p_sum.astype(jnp.bfloat16)
    rem = p_sum - p_hi.astype(jnp.float32)
    p_mid = rem.astype(jnp.bfloat16)
    p_lo = (rem - p_mid.astype(jnp.float32)).astype(jnp.bfloat16)
    imp = _dot(wimp, p_hi) + _dot(wimp, p_mid) + _dot(wimp, p_lo)

    nsb = imp.shape[0]
    jrow = _iota((nsb, TQ), 0)
    cur = lax.shift_right_logical(q0 + _iota((nsb, TQ), 1), 6)
    causal = jrow <= cur
    forced = (jrow == 0) | (jrow == cur) | (jrow == cur - 1)
    score = jnp.where(causal, jnp.where(forced, BIG, imp), NEG)
    rank = _selection_rank(score)
    unselected = jnp.where(causal, jnp.where(rank < float(N_SELECT), 0.0, NEG_FEATURE), NEG_FEATURE)
    mask_feat = jnp.concatenate([unselected, jnp.zeros((FEAT - nsb, TQ), jnp.float32)], axis=0).T
    q_sel = jnp.concatenate(
        [q, jnp.concatenate([mask_feat.astype(jnp.bfloat16)] * GROUP, axis=0)], axis=1)

    tk = TK_SLC
    diag_pair = (q0 + TQ - 1) // (2 * tk)

    def slc_scores(j):
        k0 = pl.multiple_of(j * tk, tk)
        k_aug = jnp.concatenate([ks_ref[0, 0, pl.ds(k0, tk), :], onehot_ref[pl.ds(k0, tk), :]], axis=1)
        return _dot_nt(k_aug, q_sel)

    def slc_update(j, slot, carry, near_diagonal):
        m, acc = carry
        k0 = pl.multiple_of(j * tk, tk)
        s = s_buf[slot]
        if near_diagonal:
            dist = (q0 + _iota((tk, TQ), 1)) - (k0 + _iota((tk, TQ), 0))
            s = s + _per_head(jnp.where(dist >= 0, 0.0, NEG))
        m_new = jnp.maximum(m, jnp.max(s, axis=0, keepdims=True))
        alpha = jnp.exp2(m - m_new)
        p = jnp.exp2(s - m_new)
        acc = alpha * acc + _dot(vst_ref[0, 0, :, pl.ds(k0, tk)], p.astype(jnp.bfloat16))
        return m_new, acc

    s_buf[2] = slc_scores(2 * diag_pair)
    s_buf[3] = slc_scores(2 * diag_pair + 1)
    s_buf[0] = slc_scores(0)

    qpos = (q0 + _iota((1, TQ), 1)).astype(jnp.float32)
    qpos_log2 = sum(qpos * piece for piece in LOG2E_PIECES)
    sink_row = jnp.concatenate(
        [sinks_ref[g * GROUP + r] * LOG2E + _head_slope(g, r) * qpos_log2 for r in range(GROUP)],
        axis=1)
    _store_heads(_band_finish(st_swa, vbt_ref, q0, SWA_WINDOW, sink_row), szb_ref, ob_ref)

    o_win = _band_finish(st_win, vwt_ref, q0, NSA_WINDOW, None)

    n = GROUP * TQ
    carry = (jnp.full((1, n), M_FLOOR, jnp.float32), jnp.zeros((V_ROWS, n), jnp.float32))
    carry = slc_update(2 * diag_pair, 2, carry, True)
    carry = slc_update(2 * diag_pair + 1, 3, carry, True)

    def slc_pair(jj, carry):
        j = 2 * jj
        s_buf[1] = slc_scores(j + 1)
        carry = slc_update(j, 0, carry, False)
        s_buf[0] = slc_scores(j + 2)
        return slc_update(j + 1, 1, carry, False)

    _, acc = lax.fori_loop(0, diag_pair, slc_pair, carry)
    o_slc = acc[:HEAD_DIM] / acc[HEAD_DIM:HEAD_DIM + 1]

    pieces = []
    for r in range(GROUP):
        sl = slice(r * TQ, (r + 1) * TQ)
        gate = [gt_ref[0, pl.ds(br * N_HEADS + g * GROUP + r, 1), :] for br in range(3)]
        pieces.append(gate[0] * o_cmp[:, sl] + gate[1] * o_slc[:, sl] + gate[2] * o_win[:, sl])
    _store_heads(jnp.concatenate(pieces, axis=1), sza_ref, oa_ref)


def _mixers(sinks, qa, qb, kc, vct, ks, vst, kw, vwt, kb, vbt, onehot, gt, sza, szb):
    B, _, _, S, _ = qa.shape
    n_cmp = kc.shape[1]
    q_spec = pl.BlockSpec((1, 1, GROUP, TQ, FEAT), lambda b, g, i: (b, g, 0, i, 0))
    seq_k = lambda n: pl.BlockSpec((1, 1, n, FEAT), lambda b, g, i: (b, g, 0, 0))
    seq_vt = lambda n: pl.BlockSpec((1, 1, V_ROWS, n), lambda b, g, i: (b, g, 0, 0))
    o_spec = pl.BlockSpec((1, TQ, GROUP * HEAD_DIM), lambda b, g, i: (b, i, g))
    o_shape = jax.ShapeDtypeStruct((B, S, WIDTH), jnp.bfloat16)
    return pl.pallas_call(
        _mixers_kernel,
        out_shape=(o_shape, o_shape),
        grid=(B, N_KV, S // TQ),
        in_specs=[pl.BlockSpec(memory_space=pltpu.SMEM), q_spec, q_spec,
                  pl.BlockSpec((1, n_cmp, FEAT), lambda b, g, i: (b * N_KV + g, 0, 0)),
                  pl.BlockSpec((1, HEAD_DIM, n_cmp), lambda b, g, i: (b * N_KV + g, 0, 0)),
                  seq_k(S), seq_vt(S), seq_k(kw.shape[2]), seq_vt(kw.shape[2]),
                  seq_k(kb.shape[2]), seq_vt(kb.shape[2]),
                  pl.BlockSpec((S, FEAT), lambda b, g, i: (0, 0)),
                  pl.BlockSpec((1, GATE_ROWS, TQ), lambda b, g, i: (b, 0, i)),
                  o_spec, o_spec],
        out_specs=(o_spec, o_spec),
        scratch_shapes=[pltpu.VMEM((4, TK_SLC, GROUP * TQ), jnp.float32)],
        compiler_params=pltpu.CompilerParams(
            dimension_semantics=("parallel", "parallel", "arbitrary"),
            vmem_limit_bytes=VMEM_LIMIT),
        name="mixers",
    )(sinks, qa, qb, kc, vct, ks, vst, kw, vwt, kb, vbt, onehot, gt, sza, szb)


def _out_proj_kernel(x_ref, oza_ref, ozb_ref, sma_ref, smb_ref, gate_ref, gpost_ref,
                     woa_ref, wob_ref, wout_ref, o_ref):
    ya = _dot(oza_ref[0], woa_ref[...])
    yb = _dot(ozb_ref[0], wob_ref[...])
    y = sma_ref[0].astype(jnp.float32) * ya + smb_ref[0].astype(jnp.float32) * yb
    yo = _dot(y.astype(jnp.bfloat16), wout_ref[...])
    ms = jnp.mean(yo * yo, axis=-1, keepdims=True)
    normed = yo * lax.rsqrt(ms + RMS_EPS) * gpost_ref[...]
    o_ref[0] = x_ref[0] + gate_ref[0] * normed


def _out_proj(x, oza, ozb, sma, smb, gate, gpost, woa, wob, wout):
    B, S, D = x.shape
    tm = TM_PROJ
    row_spec = lambda n: pl.BlockSpec((1, tm, n), lambda b, i: (b, i, 0))
    const = lambda shape: pl.BlockSpec(shape, lambda b, i: (0,) * len(shape))
    return pl.pallas_call(
        _out_proj_kernel,
        out_shape=jax.ShapeDtypeStruct((B, S, D), jnp.float32),
        grid=(B, S // tm),
        in_specs=[row_spec(D), row_spec(WIDTH), row_spec(WIDTH), row_spec(D), row_spec(D),
                  pl.BlockSpec((1, 1, D), lambda b, i: (b, 0, 0)), const((1, D)),
                  const((WIDTH, D)), const((WIDTH, D)), const((D, D))],
        out_specs=row_spec(D),
        compiler_params=pltpu.CompilerParams(
            dimension_semantics=("parallel", "parallel"), vmem_limit_bytes=VMEM_LIMIT),
        name="out_proj",
    )(x, oza, ozb, sma, smb, gate, gpost, woa, wob, wout)


def _arrange_w_in(w):
    o = 0
    seg = {}
    for name, n in (("qa", WIDTH), ("kvc", 2 * KVW), ("kvs", 2 * KVW), ("kvw", 2 * KVW),
                    ("g", N_GATES), ("za", WIDTH), ("qb", WIDTH), ("kvb", 2 * KVW),
                    ("zb", WIDTH), ("m", 2 * D_MODEL)):
        seg[name] = w[:, o:o + n]
        o += n
    scale = HEAD_DIM ** -0.5 * LOG2E
    gpad = jnp.pad(seg["g"], ((0, 0), (0, GATE_PAD - N_GATES)))
    cols = [seg["qa"] * scale, seg["qb"] * scale, seg["kvc"], seg["kvs"], seg["kvw"], seg["kvb"],
            seg["za"], seg["zb"], seg["m"], gpad]
    return jnp.concatenate(cols, axis=1).astype(jnp.bfloat16)


def _front_pad(k, vt, n):
    B, G = k.shape[:2]
    pad_row = jnp.zeros((FEAT,), jnp.float32).at[LANE_BLK:LANE_POS].set(PAD_KEY_BLOCK).astype(k.dtype)
    k_pad = jnp.broadcast_to(pad_row, (B, G, n, FEAT))
    vt_pad = jnp.zeros((B, G, V_ROWS, n), vt.dtype)
    return jnp.concatenate([k_pad, k], axis=2), jnp.concatenate([vt_pad, vt], axis=3)


def _block_onehot(S):
    blk = jnp.arange(S, dtype=jnp.int32)[:, None] // SLC_BLOCK
    return (blk == jnp.arange(FEAT, dtype=jnp.int32)[None, :]).astype(jnp.bfloat16)


def kernel(x, c, w_ada, b_ada, g_pre, g_post, w_in, pe_cmp_k, pe_cmp_v, w_cmp_k1, w_cmp_k2,
           w_cmp_v1, w_cmp_v2, w_o_nsa, w_o_swa, w_out, sinks):
    B, S, D = x.shape
    depth = w_in.shape[0]
    bf = jnp.bfloat16
    for l in range(depth):
        c8 = jnp.pad(c, ((0, 8 - B), (0, 0)))
        mod = _adaln_mod(c8, w_ada[l], b_ada[l][None, :])[:B]
        shift, scale, gate = (mod[:, None, i * D:(i + 1) * D] for i in range(3))

        (qa, qb, kc_raw, vc_raw, ks, vst, kw, vwt, kb, vbt, sza, szb, sma, smb, gt) = _in_proj(
            x, shift, scale, g_pre[l][None, :], _arrange_w_in(w_in[l]))

        rows = S // CMP_STRIDE
        t = jnp.stack([kc_raw, vc_raw]).reshape(2, B * N_KV, rows, CMP_STRIDE * HEAD_DIM)
        pe = jnp.stack([pe_cmp_k[l], pe_cmp_v[l]]).reshape(2, 1, CMP_BLOCK * HEAD_DIM)
        w2 = jnp.pad(jnp.stack([w_cmp_k2[l], w_cmp_v2[l]]), ((0, 0), (0, 0), (0, FEAT - HEAD_DIM)))
        cmp_o, cmp_ot = _compress(t, pe, jnp.stack([w_cmp_k1[l], w_cmp_v1[l]]), w2)

        kw, vwt = _front_pad(kw, vwt, NSA_WINDOW)
        kb, vbt = _front_pad(kb, vbt, SWA_WINDOW)
        oza, ozb = _mixers(sinks[l], qa, qb, cmp_o[0], cmp_ot[1], ks, vst, kw, vwt, kb, vbt,
                           _block_onehot(S), gt, sza, szb)

        x = _out_proj(x, oza, ozb, sma, smb, gate, g_post[l][None, :],
                      w_o_nsa[l].astype(bf), w_o_swa[l].astype(bf), w_out[l].astype(bf))
    return x
```

```python
import jax
import jax.numpy as jnp
import numpy as np
from jax import lax
from jax.experimental import pallas as pl
from jax.experimental.pallas import tpu as pltpu

D_MODEL = 1024
HEAD_DIM = 64
N_HEADS = 8
N_KV = 2
GROUP = N_HEADS // N_KV
CMP_BLOCK = 32
CMP_STRIDE = 16
CMP_HIDDEN = 256
SLC_BLOCK = 64
N_SELECT = 16
NSA_WINDOW = 512
SWA_WINDOW = 128
RMS_EPS = 1e-6
NEG = -1e30
BIG = 1e30
M_FLOOR = -1e20
NEG_FEATURE = -(2.0 ** 100)
PAD_KEY_BLOCK = -(2.0 ** 90)

WIDTH = N_HEADS * HEAD_DIM
KVW = N_KV * HEAD_DIM
FEAT = 128
V_ROWS = 80


def _bf16_pieces(x, n):
    out = []
    for _ in range(n):
        bits = np.array(x, np.float32).view(np.uint32)
        bits = (bits + np.uint32(0x7FFF) + ((bits >> np.uint32(16)) & np.uint32(1))) & np.uint32(0xFFFF0000)
        piece = float(bits.view(np.float32))
        out.append(piece)
        x -= piece
    return tuple(out)


LOG2E_PIECES = _bf16_pieces(1.4426950408889634, 3)
LOG2E = sum(LOG2E_PIECES)
N_PIECES = len(LOG2E_PIECES)
LANE_BLK = HEAD_DIM
LANE_POS = HEAD_DIM + N_PIECES
N_GATES = 3 * N_HEADS
GATE_PAD = 128
GATE_ROWS = 32

VMEM_LIMIT = 56 * 1024 * 1024

TM_PROJ = 512
TQ = 256
TK_SLC = 512
CHUNK = 16

_NT = (((1,), (1,)), ((), ()))


def _dot(a, b):
    return jnp.dot(a, b, preferred_element_type=jnp.float32)


def _dot_nt(a, b):
    return lax.dot_general(a, b, _NT, preferred_element_type=jnp.float32)


def _sigmoid(v):
    return 1.0 / (1.0 + jnp.exp(-v))


def _iota(shape, dim):
    return lax.broadcasted_iota(jnp.int32, shape, dim)


def _key_features(lane, pos):
    blk = lax.shift_right_logical(pos, 6).astype(jnp.float32)
    within = (pos & (SLC_BLOCK - 1)).astype(jnp.float32)
    return jnp.where(lane < LANE_POS, blk, jnp.where(lane < LANE_POS + N_PIECES, within, 0.0))


def _query_features(lane, slope):
    feat = jnp.zeros(lane.shape, jnp.float32)
    for i, piece in enumerate(LOG2E_PIECES):
        feat = jnp.where(lane == LANE_BLK + i, slope * SLC_BLOCK * piece, feat)
        feat = jnp.where(lane == LANE_POS + i, slope * piece, feat)
    return feat


def _adaln_kernel(c_ref, w_ref, b_ref, o_ref):
    o_ref[...] = jnp.dot(c_ref[...], w_ref[...], preferred_element_type=jnp.float32,
                         precision=lax.Precision.HIGHEST) + b_ref[...]


def _adaln_mod(c8, w, b):
    n = w.shape[1]
    bn = 1024
    return pl.pallas_call(
        _adaln_kernel,
        out_shape=jax.ShapeDtypeStruct((c8.shape[0], n), jnp.float32),
        grid=(n // bn,),
        in_specs=[pl.BlockSpec((c8.shape[0], D_MODEL), lambda j: (0, 0)),
                  pl.BlockSpec((D_MODEL, bn), lambda j: (0, j)),
                  pl.BlockSpec((1, bn), lambda j: (0, j))],
        out_specs=pl.BlockSpec((c8.shape[0], bn), lambda j: (0, j)),
        name="adaln_mod",
    )(c8, w, b)


_C_Q = 0
_C_KV = 2 * WIDTH
_C_Z = _C_KV + 8 * KVW
_C_M = _C_Z + 2 * WIDTH
_C_G = _C_M + 2 * D_MODEL
_C_END = _C_G + GATE_PAD


def _with_features(pair, odd, feat, lane):
    src = pltpu.roll(pair, HEAD_DIM, 1) if odd else pair
    return jnp.where(lane < HEAD_DIM, src, feat).astype(jnp.bfloat16)


def _in_proj_kernel(x_ref, shift_ref, scale_ref, gpre_ref, w_ref,
                    qa_ref, qb_ref, kc_ref, vc_ref, ks_ref, vst_ref, kw_ref, vwt_ref,
                    kb_ref, vbt_ref, sza_ref, szb_ref, sma_ref, smb_ref, gt_ref):
    tm = x_ref.shape[1]
    xf = x_ref[0]
    ms = jnp.mean(xf * xf, axis=-1, keepdims=True)
    y = xf * lax.rsqrt(ms + RMS_EPS) * gpre_ref[...]
    h = y * (1.0 + scale_ref[0]) + shift_ref[0]
    hb = h.astype(jnp.bfloat16)

    lane = _iota((tm, FEAT), 1)
    pos = pl.program_id(1) * tm + _iota((tm, FEAT), 0)
    key_feat = _key_features(lane, pos)

    acc = _dot(hb, w_ref[:, _C_Q:_C_KV])
    for head in range(N_HEADS):
        q_feat = _query_features(lane, 2.0 ** -(head + 1))
        g, r, odd = head // GROUP, head % GROUP, head % 2
        c = (head // 2) * FEAT
        qa_ref[0, g, r] = _with_features(acc[:, c:c + FEAT], odd, q_feat, lane)
        qb_ref[0, g, r] = _with_features(acc[:, WIDTH + c:WIDTH + c + FEAT], odd, q_feat, lane)

    acc = _dot(hb, w_ref[:, _C_KV:_C_Z])
    for g in range(N_KV):
        c = g * HEAD_DIM
        kc_ref[0, g] = acc[:, c:c + HEAD_DIM].astype(jnp.bfloat16)
        vc_ref[0, g] = acc[:, KVW + c:KVW + c + HEAD_DIM].astype(jnp.bfloat16)
        ks_ref[0, g] = _with_features(acc[:, 2 * KVW:3 * KVW], g, key_feat, lane)
        kw_ref[0, g] = _with_features(acc[:, 4 * KVW:5 * KVW], g, key_feat, lane)
        kb_ref[0, g] = _with_features(acc[:, 6 * KVW:7 * KVW], g, key_feat, lane)
    vst = acc[:, 3 * KVW:4 * KVW].T
    vwt = acc[:, 5 * KVW:6 * KVW].T
    vbt = acc[:, 7 * KVW:8 * KVW].T
    ones_rows = jnp.where(_iota((V_ROWS - HEAD_DIM, tm), 0) == 0, 1.0, 0.0).astype(jnp.bfloat16)
    for g in range(N_KV):
        c = g * HEAD_DIM
        for ref, vt in ((vst_ref, vst), (vwt_ref, vwt), (vbt_ref, vbt)):
            ref[0, g] = jnp.concatenate([vt[c:c + HEAD_DIM].astype(jnp.bfloat16), ones_rows], axis=0)

    acc = _dot(hb, w_ref[:, _C_Z:_C_M])
    sz = acc * _sigmoid(acc)
    sza_ref[0] = sz[:, :WIDTH].astype(jnp.bfloat16)
    szb_ref[0] = sz[:, WIDTH:].astype(jnp.bfloat16)

    acc = _dot(hb, w_ref[:, _C_M:_C_M + D_MODEL])
    sma_ref[0] = _sigmoid(acc).astype(jnp.bfloat16)
    acc = _dot(hb, w_ref[:, _C_M + D_MODEL:_C_G])
    smb_ref[0] = _sigmoid(acc).astype(jnp.bfloat16)

    acc = _dot(hb, w_ref[:, _C_G:_C_END])
    gt_ref[0] = _sigmoid(acc).T[:GATE_ROWS]


def _in_proj(x, shift, scale, gpre, w):
    B, S, D = x.shape
    tm = TM_PROJ
    bf = jnp.bfloat16
    q_shape = jax.ShapeDtypeStruct((B, N_KV, GROUP, S, FEAT), bf)
    kraw_shape = jax.ShapeDtypeStruct((B, N_KV, S, HEAD_DIM), bf)
    k_shape = jax.ShapeDtypeStruct((B, N_KV, S, FEAT), bf)
    vt_shape = jax.ShapeDtypeStruct((B, N_KV, V_ROWS, S), bf)
    q_spec = pl.BlockSpec((1, N_KV, GROUP, tm, FEAT), lambda b, i: (b, 0, 0, i, 0))
    kraw_spec = pl.BlockSpec((1, N_KV, tm, HEAD_DIM), lambda b, i: (b, 0, i, 0))
    k_spec = pl.BlockSpec((1, N_KV, tm, FEAT), lambda b, i: (b, 0, i, 0))
    vt_spec = pl.BlockSpec((1, N_KV, V_ROWS, tm), lambda b, i: (b, 0, 0, i))
    row_spec = lambda n: pl.BlockSpec((1, tm, n), lambda b, i: (b, i, 0))
    vec_spec = pl.BlockSpec((1, 1, D), lambda b, i: (b, 0, 0))
    return pl.pallas_call(
        _in_proj_kernel,
        out_shape=(q_shape, q_shape, kraw_shape, kraw_shape, k_shape, vt_shape, k_shape, vt_shape,
                   k_shape, vt_shape,
                   jax.ShapeDtypeStruct((B, S, WIDTH), bf), jax.ShapeDtypeStruct((B, S, WIDTH), bf),
                   jax.ShapeDtypeStruct((B, S, D), bf), jax.ShapeDtypeStruct((B, S, D), bf),
                   jax.ShapeDtypeStruct((B, GATE_ROWS, S), jnp.float32)),
        grid=(B, S // tm),
        in_specs=[row_spec(D), vec_spec, vec_spec,
                  pl.BlockSpec((1, D), lambda b, i: (0, 0)),
                  pl.BlockSpec((D, _C_END), lambda b, i: (0, 0), pipeline_mode=pl.Buffered(1))],
        out_specs=(q_spec, q_spec, kraw_spec, kraw_spec, k_spec, vt_spec, k_spec, vt_spec,
                   k_spec, vt_spec,
                   row_spec(WIDTH), row_spec(WIDTH), row_spec(D), row_spec(D),
                   pl.BlockSpec((1, GATE_ROWS, tm), lambda b, i: (b, 0, i))),
        compiler_params=pltpu.CompilerParams(
            dimension_semantics=("parallel", "parallel"), vmem_limit_bytes=VMEM_LIMIT),
        name="in_proj",
    )(x, shift, scale, gpre, w)


def _compress_kernel(t_ref, pe_ref, w1_ref, w2_ref, o_ref, ot_ref):
    half = CMP_STRIDE * HEAD_DIM
    t = t_ref[0, 0]
    w1 = w1_ref[0]
    w1b = w1.astype(jnp.bfloat16)
    first = _dot(t, w1b[:half])
    second = _dot(t, w1b[half:])
    n = second.shape[0]
    second = pltpu.roll(second, n - 1, 0)
    pe = jnp.broadcast_to(pe_ref[0], (8, 2 * half))
    bias = jnp.dot(pe, w1, preferred_element_type=jnp.float32,
                   precision=lax.Precision.HIGHEST)[0:1]
    pre = first + second + bias
    hid = pre * _sigmoid(pre)
    out = _dot(hid.astype(jnp.bfloat16), w2_ref[0].astype(jnp.bfloat16))
    lane = _iota((n, FEAT), 1)
    c = _iota((n, FEAT), 0)
    end = c * CMP_STRIDE + (CMP_BLOCK - 1)
    o_ref[0, 0] = jnp.where(lane < HEAD_DIM, out, _key_features(lane, end)).astype(jnp.bfloat16)
    ot_ref[0, 0] = out.T[:HEAD_DIM].astype(jnp.bfloat16)


def _compress(t, pe, w1, w2):
    _, BG, n, wdt = t.shape
    return pl.pallas_call(
        _compress_kernel,
        out_shape=(jax.ShapeDtypeStruct((2, BG, n, FEAT), jnp.bfloat16),
                   jax.ShapeDtypeStruct((2, BG, HEAD_DIM, n), jnp.bfloat16)),
        grid=(2, BG),
        in_specs=[pl.BlockSpec((1, 1, n, wdt), lambda a, i: (a, i, 0, 0)),
                  pl.BlockSpec((1, 1, 2 * wdt), lambda a, i: (a, 0, 0)),
                  pl.BlockSpec((1, 2 * wdt, CMP_HIDDEN), lambda a, i: (a, 0, 0)),
                  pl.BlockSpec((1, CMP_HIDDEN, FEAT), lambda a, i: (a, 0, 0))],
        out_specs=(pl.BlockSpec((1, 1, n, FEAT), lambda a, i: (a, i, 0, 0)),
                   pl.BlockSpec((1, 1, HEAD_DIM, n), lambda a, i: (a, i, 0, 0))),
        compiler_params=pltpu.CompilerParams(dimension_semantics=("parallel", "parallel")),
        name="compress",
    )(t, pe, w1, w2)


def _head_slope(g, r):
    return jnp.where(g == 0, 1.0, 2.0 ** -GROUP).astype(jnp.float32) * (2.0 ** -(r + 1))


def _chunked_max(rows, n_rows):
    m8 = None
    for c in range(0, n_rows, CHUNK):
        chunk = rows(c)
        for r in range(0, CHUNK, 8):
            m8 = chunk[r:r + 8] if m8 is None else jnp.maximum(m8, chunk[r:r + 8])
    return jnp.max(m8, axis=0, keepdims=True)


def _chunked_exp2(rows, n_rows, m):
    return jnp.concatenate(
        [jnp.exp2(rows(c) - m).astype(jnp.bfloat16) for c in range(0, n_rows, CHUNK)], axis=0)


def _per_head(x):
    return jnp.concatenate([x] * GROUP, axis=1)


def _band_scores(q, k_ref, q0, window):
    return _dot_nt(k_ref[0, 0, pl.ds(pl.multiple_of(q0, TQ), window + TQ), :], q)


def _band_finish(st, vt_ref, q0, window, sink_row):
    n_keys = window + TQ
    start = pl.multiple_of(q0, TQ)
    col = _iota((CHUNK, TQ), 1)

    def rows(c):
        part = st[c:c + CHUNK]
        row = c + _iota((CHUNK, TQ), 0)
        if c < TQ:
            part = part + _per_head(jnp.where(row > col, 0.0, NEG))
        if c + CHUNK - 1 > window:
            part = part + _per_head(jnp.where(row <= col + window, 0.0, NEG))
        return part

    m = _chunked_max(rows, n_keys)
    if sink_row is not None:
        m = jnp.maximum(m, sink_row)
    p = _chunked_exp2(rows, n_keys, m)
    o = _dot(vt_ref[0, 0, :, pl.ds(start, n_keys)], p)
    l = o[HEAD_DIM:HEAD_DIM + 1]
    if sink_row is not None:
        l = l + jnp.exp2(sink_row - m)
    return o[:HEAD_DIM] / l


def _store_heads(o_t, sz_ref, o_ref):
    stacked = jnp.concatenate([o_t[:, r * TQ:(r + 1) * TQ] for r in range(GROUP)], axis=0)
    o_ref[0] = (stacked.T * sz_ref[0].astype(jnp.float32)).astype(jnp.bfloat16)


def _importance_matrix(nsb, n_cmp):
    ratio = SLC_BLOCK // CMP_STRIDE
    span = CMP_BLOCK // CMP_STRIDE
    off = ratio * np.arange(nsb)[:, None] - np.arange(n_cmp)[None, :]
    w = sum((off == mm + nn).astype(np.float32) for mm in range(ratio) for nn in range(span))
    w[:, n_cmp - 1] = 0.0
    return jnp.asarray(w, jnp.bfloat16)


def _selection_rank(score):
    nsb = score.shape[0]
    groups = [score[8 * a:8 * a + 8] for a in range(nsb // 8)]
    sub = _iota((8, TQ), 0)
    rank = [jnp.zeros((8, TQ), jnp.float32) for _ in groups]
    for i in range(nsb):
        row = jnp.broadcast_to(score[i:i + 1, :], (8, TQ))
        for a, grp in enumerate(groups):
            if a < i // 8:
                ahead = jnp.where(row > grp, 1.0, 0.0)
            elif a > i // 8:
                ahead = jnp.where(row >= grp, 1.0, 0.0)
            else:
                ahead = jnp.where(sub > i % 8, jnp.where(row >= grp, 1.0, 0.0),
                                  jnp.where(row > grp, 1.0, 0.0))
            rank[a] = rank[a] + ahead
    return jnp.concatenate(rank, axis=0)


def _mixers_kernel(sinks_ref, qa_ref, qb_ref, kc_ref, vct_ref, ks_ref, vst_ref, kw_ref, vwt_ref,
                   kb_ref, vbt_ref, onehot_ref, wimp_ref, gt_ref, sza_ref, szb_ref, oa_ref, ob_ref,
                   s_buf):
    g = pl.program_id(1)
    qi = pl.program_id(2)
    q0 = qi * TQ
    q = qa_ref[0, 0].reshape(GROUP * TQ, FEAT)

    n_cmp = kc_ref.shape[1]
    st_cmp = _dot_nt(kc_ref[0], q)
    st_swa = _band_scores(qb_ref[0, 0].reshape(GROUP * TQ, FEAT), kb_ref, q0, SWA_WINDOW)
    st_win = _band_scores(q, kw_ref, q0, NSA_WINDOW)

    qpos_c = q0 + _iota((n_cmp, TQ), 1)
    end_c = _iota((n_cmp, TQ), 0) * CMP_STRIDE + (CMP_BLOCK - 1)
    hidden = jnp.where(qpos_c >= end_c, 0.0, NEG)
    any_valid = (qpos_c[0:1] >= CMP_BLOCK - 1).astype(jnp.float32)
    p_sum = jnp.zeros((n_cmp, TQ), jnp.float32)
    o_cmp = []
    for r in range(GROUP):
        sr = st_cmp[:, r * TQ:(r + 1) * TQ] + hidden
        er = jnp.exp2(sr - jnp.max(sr, axis=0, keepdims=True))
        inv = any_valid / jnp.sum(er, axis=0, keepdims=True)
        p_sum = p_sum + er * inv
        o_cmp.append(_dot(vct_ref[0], er.astype(jnp.bfloat16)) * inv)
    o_cmp = jnp.concatenate(o_cmp, axis=1)

    wimp = wimp_ref[...]
    p_hi = p_sum.astype(jnp.bfloat16)
    rem = p_sum - p_hi.astype(jnp.float32)
    p_mid = rem.astype(jnp.bfloat16)
    p_lo = (rem - p_mid.astype(jnp.float32)).astype(jnp.bfloat16)
    imp = _dot(wimp, p_hi) + _dot(wimp, p_mid) + _dot(wimp, p_lo)

    qpos = (q0 + _iota((1, TQ), 1)).astype(jnp.float32)
    qpos_log2 = sum(qpos * piece for piece in LOG2E_PIECES)
    sink_row = jnp.concatenate(
        [sinks_ref[g * GROUP + r] * LOG2E + _head_slope(g, r) * qpos_log2 for r in range(GROUP)],
        axis=1)
    _store_heads(_band_finish(st_swa, vbt_ref, q0, SWA_WINDOW, sink_row), szb_ref, ob_ref)

    o_win = _band_finish(st_win, vwt_ref, q0, NSA_WINDOW, None)

    nsb = imp.shape[0]
    jrow = _iota((nsb, TQ), 0)
    cur = lax.shift_right_logical(q0 + _iota((nsb, TQ), 1), 6)
    causal = jrow <= cur
    forced = (jrow == 0) | (jrow == cur) | (jrow == cur - 1)
    score = jnp.where(causal, jnp.where(forced, BIG, imp), NEG)
    rank = _selection_rank(score)
    unselected = jnp.where(causal, jnp.where(rank < float(N_SELECT), 0.0, NEG_FEATURE), NEG_FEATURE)
    mask_feat = jnp.concatenate([unselected, jnp.zeros((FEAT - nsb, TQ), jnp.float32)], axis=0).T
    q_sel = jnp.concatenate(
        [q, jnp.concatenate([mask_feat.astype(jnp.bfloat16)] * GROUP, axis=0)], axis=1)

    tk = TK_SLC
    diag_pair = (q0 + TQ - 1) // (2 * tk)

    def slc_scores(j):
        k0 = pl.multiple_of(j * tk, tk)
        k_aug = jnp.concatenate([ks_ref[0, 0, pl.ds(k0, tk), :], onehot_ref[pl.ds(k0, tk), :]], axis=1)
        return _dot_nt(k_aug, q_sel)

    def slc_update(j, slot, carry, near_diagonal):
        m, acc = carry
        k0 = pl.multiple_of(j * tk, tk)

        def rows(c):
            s = s_buf[slot, c:c + CHUNK, :]
            if near_diagonal:
                dist = (q0 + _iota((CHUNK, TQ), 1)) - (k0 + c + _iota((CHUNK, TQ), 0))
                s = s + _per_head(jnp.where(dist >= 0, 0.0, NEG))
            return s

        m_new = jnp.maximum(m, _chunked_max(rows, tk))
        alpha = jnp.exp2(m - m_new)
        p = _chunked_exp2(rows, tk, m_new)
        acc = alpha * acc + _dot(vst_ref[0, 0, :, pl.ds(k0, tk)], p)
        return m_new, acc

    s_buf[2] = slc_scores(2 * diag_pair)
    s_buf[3] = slc_scores(2 * diag_pair + 1)
    s_buf[0] = slc_scores(0)

    n = GROUP * TQ
    carry = (jnp.full((1, n), M_FLOOR, jnp.float32), jnp.zeros((V_ROWS, n), jnp.float32))
    carry = slc_update(2 * diag_pair, 2, carry, True)
    carry = slc_update(2 * diag_pair + 1, 3, carry, True)

    def slc_pair(jj, carry):
        j = 2 * jj
        s_buf[1] = slc_scores(j + 1)
        carry = slc_update(j, 0, carry, False)
        s_buf[0] = slc_scores(j + 2)
        return slc_update(j + 1, 1, carry, False)

    _, acc = lax.fori_loop(0, diag_pair, slc_pair, carry)
    o_slc = acc[:HEAD_DIM] / acc[HEAD_DIM:HEAD_DIM + 1]

    pieces = []
    for r in range(GROUP):
        sl = slice(r * TQ, (r + 1) * TQ)
        gate = [gt_ref[0, pl.ds(br * N_HEADS + g * GROUP + r, 1), :] for br in range(3)]
        pieces.append(gate[0] * o_cmp[:, sl] + gate[1] * o_slc[:, sl] + gate[2] * o_win[:, sl])
    _store_heads(jnp.concatenate(pieces, axis=1), sza_ref, oa_ref)


def _mixers(sinks, qa, qb, kc, vct, ks, vst, kw, vwt, kb, vbt, onehot, wimp, gt, sza, szb):
    B, _, _, S, _ = qa.shape
    n_cmp = kc.shape[1]
    q_spec = pl.BlockSpec((1, 1, GROUP, TQ, FEAT), lambda b, g, i: (b, g, 0, i, 0))
    seq_k = lambda n: pl.BlockSpec((1, 1, n, FEAT), lambda b, g, i: (b, g, 0, 0))
    seq_vt = lambda n: pl.BlockSpec((1, 1, V_ROWS, n), lambda b, g, i: (b, g, 0, 0))
    o_spec = pl.BlockSpec((1, TQ, GROUP * HEAD_DIM), lambda b, g, i: (b, i, g))
    o_shape = jax.ShapeDtypeStruct((B, S, WIDTH), jnp.bfloat16)
    return pl.pallas_call(
        _mixers_kernel,
        out_shape=(o_shape, o_shape),
        grid=(B, N_KV, S // TQ),
        in_specs=[pl.BlockSpec(memory_space=pltpu.SMEM), q_spec, q_spec,
                  pl.BlockSpec((1, n_cmp, FEAT), lambda b, g, i: (b * N_KV + g, 0, 0)),
                  pl.BlockSpec((1, HEAD_DIM, n_cmp), lambda b, g, i: (b * N_KV + g, 0, 0)),
                  seq_k(S), seq_vt(S), seq_k(kw.shape[2]), seq_vt(kw.shape[2]),
                  seq_k(kb.shape[2]), seq_vt(kb.shape[2]),
                  pl.BlockSpec((S, FEAT), lambda b, g, i: (0, 0)),
                  pl.BlockSpec(wimp.shape, lambda b, g, i: (0, 0)),
                  pl.BlockSpec((1, GATE_ROWS, TQ), lambda b, g, i: (b, 0, i)),
                  o_spec, o_spec],
        out_specs=(o_spec, o_spec),
        scratch_shapes=[pltpu.VMEM((4, TK_SLC, GROUP * TQ), jnp.float32)],
        compiler_params=pltpu.CompilerParams(
            dimension_semantics=("parallel", "parallel", "arbitrary"),
            vmem_limit_bytes=VMEM_LIMIT),
        name="mixers",
    )(sinks, qa, qb, kc, vct, ks, vst, kw, vwt, kb, vbt, onehot, wimp, gt, sza, szb)


def _out_proj_kernel(x_ref, oza_ref, ozb_ref, sma_ref, smb_ref, gate_ref, gpost_ref,
                     woa_ref, wob_ref, wout_ref, o_ref):
    ya = _dot(oza_ref[0], woa_ref[...])
    yb = _dot(ozb_ref[0], wob_ref[...])
    y = sma_ref[0].astype(jnp.float32) * ya + smb_ref[0].astype(jnp.float32) * yb
    yo = _dot(y.astype(jnp.bfloat16), wout_ref[...])
    ms = jnp.mean(yo * yo, axis=-1, keepdims=True)
    normed = yo * lax.rsqrt(ms + RMS_EPS) * gpost_ref[...]
    o_ref[0] = x_ref[0] + gate_ref[0] * normed


def _out_proj(x, oza, ozb, sma, smb, gate, gpost, woa, wob, wout):
    B, S, D = x.shape
    tm = TM_PROJ
    row_spec = lambda n: pl.BlockSpec((1, tm, n), lambda b, i: (b, i, 0))
    const = lambda shape: pl.BlockSpec(shape, lambda b, i: (0,) * len(shape))
    return pl.pallas_call(
        _out_proj_kernel,
        out_shape=jax.ShapeDtypeStruct((B, S, D), jnp.float32),
        grid=(B, S // tm),
        in_specs=[row_spec(D), row_spec(WIDTH), row_spec(WIDTH), row_spec(D), row_spec(D),
                  pl.BlockSpec((1, 1, D), lambda b, i: (b, 0, 0)), const((1, D)),
                  const((WIDTH, D)), const((WIDTH, D)), const((D, D))],
        out_specs=row_spec(D),
        compiler_params=pltpu.CompilerParams(
            dimension_semantics=("parallel", "parallel"), vmem_limit_bytes=VMEM_LIMIT),
        name="out_proj",
    )(x, oza, ozb, sma, smb, gate, gpost, woa, wob, wout)


def _arrange_w_in(w):
    o = 0
    seg = {}
    for name, n in (("qa", WIDTH), ("kvc", 2 * KVW), ("kvs", 2 * KVW), ("kvw", 2 * KVW),
                    ("g", N_GATES), ("za", WIDTH), ("qb", WIDTH), ("kvb", 2 * KVW),
                    ("zb", WIDTH), ("m", 2 * D_MODEL)):
        seg[name] = w[:, o:o + n]
        o += n
    scale = HEAD_DIM ** -0.5 * LOG2E
    gpad = jnp.pad(seg["g"], ((0, 0), (0, GATE_PAD - N_GATES)))
    cols = [seg["qa"] * scale, seg["qb"] * scale, seg["kvc"], seg["kvs"], seg["kvw"], seg["kvb"],
            seg["za"], seg["zb"], seg["m"], gpad]
    return jnp.concatenate(cols, axis=1).astype(jnp.bfloat16)


def _front_pad(k, vt, n):
    B, G = k.shape[:2]
    pad_row = jnp.zeros((FEAT,), jnp.float32).at[LANE_BLK:LANE_POS].set(PAD_KEY_BLOCK).astype(k.dtype)
    k_pad = jnp.broadcast_to(pad_row, (B, G, n, FEAT))
    vt_pad = jnp.zeros((B, G, V_ROWS, n), vt.dtype)
    return jnp.concatenate([k_pad, k], axis=2), jnp.concatenate([vt_pad, vt], axis=3)


def _block_onehot(S):
    blk = jnp.arange(S, dtype=jnp.int32)[:, None] // SLC_BLOCK
    return (blk == jnp.arange(FEAT, dtype=jnp.int32)[None, :]).astype(jnp.bfloat16)


def kernel(x, c, w_ada, b_ada, g_pre, g_post, w_in, pe_cmp_k, pe_cmp_v, w_cmp_k1, w_cmp_k2,
           w_cmp_v1, w_cmp_v2, w_o_nsa, w_o_swa, w_out, sinks):
    B, S, D = x.shape
    depth = w_in.shape[0]
    bf = jnp.bfloat16
    for l in range(depth):
        c8 = jnp.pad(c, ((0, 8 - B), (0, 0)))
        mod = _adaln_mod(c8, w_ada[l], b_ada[l][None, :])[:B]
        shift, scale, gate = (mod[:, None, i * D:(i + 1) * D] for i in range(3))

        (qa, qb, kc_raw, vc_raw, ks, vst, kw, vwt, kb, vbt, sza, szb, sma, smb, gt) = _in_proj(
            x, shift, scale, g_pre[l][None, :], _arrange_w_in(w_in[l]))

        rows = S // CMP_STRIDE
        t = jnp.stack([kc_raw, vc_raw]).reshape(2, B * N_KV, rows, CMP_STRIDE * HEAD_DIM)
        pe = jnp.stack([pe_cmp_k[l], pe_cmp_v[l]]).reshape(2, 1, CMP_BLOCK * HEAD_DIM)
        w2 = jnp.pad(jnp.stack([w_cmp_k2[l], w_cmp_v2[l]]), ((0, 0), (0, 0), (0, FEAT - HEAD_DIM)))
        cmp_o, cmp_ot = _compress(t, pe, jnp.stack([w_cmp_k1[l], w_cmp_v1[l]]), w2)

        kw, vwt = _front_pad(kw, vwt, NSA_WINDOW)
        kb, vbt = _front_pad(kb, vbt, SWA_WINDOW)
        oza, ozb = _mixers(sinks[l], qa, qb, cmp_o[0], cmp_ot[1], ks, vst, kw, vwt, kb, vbt,
                           _block_onehot(S), _importance_matrix(S // SLC_BLOCK, rows), gt, sza, szb)

        x = _out_proj(x, oza, ozb, sma, smb, gate, g_post[l][None, :],
                      w_o_nsa[l].astype(bf), w_o_swa[l].astype(bf), w_out[l].astype(bf))
    return x
```

```python
import jax
import jax.numpy as jnp
import numpy as np
from jax import lax
from jax.experimental import pallas as pl
from jax.experimental.pallas import tpu as pltpu

D_MODEL = 1024
HEAD_DIM = 64
N_HEADS = 8
N_KV = 2
GROUP = N_HEADS // N_KV
CMP_BLOCK = 32
CMP_STRIDE = 16
CMP_HIDDEN = 256
SLC_BLOCK = 64
N_SELECT = 16
NSA_WINDOW = 512
SWA_WINDOW = 128
RMS_EPS = 1e-6
NEG = -1e30
BIG = 1e30
M_FLOOR = -1e20
NEG_FEATURE = -(2.0 ** 100)
PAD_KEY_BLOCK = -(2.0 ** 90)

WIDTH = N_HEADS * HEAD_DIM
KVW = N_KV * HEAD_DIM
FEAT = 128
V_ROWS = 80


def _bf16_pieces(x, n):
    out = []
    for _ in range(n):
        bits = np.array(x, np.float32).view(np.uint32)
        bits = (bits + np.uint32(0x7FFF) + ((bits >> np.uint32(16)) & np.uint32(1))) & np.uint32(0xFFFF0000)
        piece = float(bits.view(np.float32))
        out.append(piece)
        x -= piece
    return tuple(out)


LOG2E_PIECES = _bf16_pieces(1.4426950408889634, 3)
LOG2E = sum(LOG2E_PIECES)
N_PIECES = len(LOG2E_PIECES)
LANE_BLK = HEAD_DIM
LANE_POS = HEAD_DIM + N_PIECES
N_GATES = 3 * N_HEADS
GATE_PAD = 128
GATE_ROWS = 32

VMEM_LIMIT = 56 * 1024 * 1024

TM_PROJ = 512
TQ = 256
TK_SLC = 512
CHUNK = 16
KEY_PAD = TM_PROJ
BAND_ALIGN = 128

_NT = (((1,), (1,)), ((), ()))


def _dot(a, b):
    return jnp.dot(a, b, preferred_element_type=jnp.float32)


def _dot_nt(a, b):
    return lax.dot_general(a, b, _NT, preferred_element_type=jnp.float32)


def _sigmoid(v):
    return 1.0 / (1.0 + jnp.exp(-v))


def _iota(shape, dim):
    return lax.broadcasted_iota(jnp.int32, shape, dim)


def _key_features(lane, pos):
    blk = lax.shift_right_logical(pos, 6).astype(jnp.float32)
    within = (pos & (SLC_BLOCK - 1)).astype(jnp.float32)
    return jnp.where(lane < LANE_POS, blk, jnp.where(lane < LANE_POS + N_PIECES, within, 0.0))


def _query_features(lane, slope):
    feat = jnp.zeros(lane.shape, jnp.float32)
    for i, piece in enumerate(LOG2E_PIECES):
        feat = jnp.where(lane == LANE_BLK + i, slope * SLC_BLOCK * piece, feat)
        feat = jnp.where(lane == LANE_POS + i, slope * piece, feat)
    return feat


def _adaln_kernel(c_ref, w_ref, b_ref, o_ref):
    o_ref[...] = jnp.dot(c_ref[...], w_ref[...], preferred_element_type=jnp.float32,
                         precision=lax.Precision.HIGHEST) + b_ref[...]


def _adaln_mod(c8, w, b):
    n = w.shape[1]
    bn = 1024
    return pl.pallas_call(
        _adaln_kernel,
        out_shape=jax.ShapeDtypeStruct((c8.shape[0], n), jnp.float32),
        grid=(n // bn,),
        in_specs=[pl.BlockSpec((c8.shape[0], D_MODEL), lambda j: (0, 0)),
                  pl.BlockSpec((D_MODEL, bn), lambda j: (0, j)),
                  pl.BlockSpec((1, bn), lambda j: (0, j))],
        out_specs=pl.BlockSpec((c8.shape[0], bn), lambda j: (0, j)),
        name="adaln_mod",
    )(c8, w, b)


_C_Q = 0
_C_KV = 2 * WIDTH
_C_Z = _C_KV + 8 * KVW
_C_M = _C_Z + 2 * WIDTH
_C_G = _C_M + 2 * D_MODEL
_C_END = _C_G + GATE_PAD


def _with_features(pair, odd, feat, lane):
    src = pltpu.roll(pair, HEAD_DIM, 1) if odd else pair
    return jnp.where(lane < HEAD_DIM, src, feat).astype(jnp.bfloat16)


def _in_proj_kernel(x_ref, shift_ref, scale_ref, gpre_ref, w_ref, _kw0, _vwt0, _kb0, _vbt0,
                    qa_ref, qb_ref, t16_ref, ks_ref, vst_ref, kw_ref, vwt_ref,
                    kb_ref, vbt_ref, sza_ref, szb_ref, sma_ref, smb_ref, gt_ref, kv_scr):
    tm = x_ref.shape[1]
    xf = x_ref[0]
    ms = jnp.mean(xf * xf, axis=-1, keepdims=True)
    y = xf * lax.rsqrt(ms + RMS_EPS) * gpre_ref[...]
    h = y * (1.0 + scale_ref[0]) + shift_ref[0]
    hb = h.astype(jnp.bfloat16)

    lane = _iota((tm, FEAT), 1)
    pos = pl.program_id(1) * tm + _iota((tm, FEAT), 0)
    key_feat = _key_features(lane, pos)

    acc = _dot(hb, w_ref[:, _C_Q:_C_KV])
    for head in range(N_HEADS):
        q_feat = _query_features(lane, 2.0 ** -(head + 1))
        g, r, odd = head // GROUP, head % GROUP, head % 2
        c = (head // 2) * FEAT
        qa_ref[0, g, r] = _with_features(acc[:, c:c + FEAT], odd, q_feat, lane)
        qb_ref[0, g, r] = _with_features(acc[:, WIDTH + c:WIDTH + c + FEAT], odd, q_feat, lane)

    acc = _dot(hb, w_ref[:, _C_KV:_C_Z])
    for a in range(2):
        kv_scr[a] = acc[:, a * KVW:(a + 1) * KVW]
        kv_scr[2 + a] = pltpu.roll(acc[:, a * KVW:(a + 1) * KVW], HEAD_DIM, 1)
    for l in range(CMP_STRIDE):
        for a in range(2):
            plain = kv_scr[a, pl.ds(l, tm // CMP_STRIDE, stride=CMP_STRIDE), :]
            rolled = kv_scr[2 + a, pl.ds(l, tm // CMP_STRIDE, stride=CMP_STRIDE), :]
            half = (l % 2) * HEAD_DIM
            for g in range(N_KV):
                tok = (plain if g % 2 == l % 2 else rolled)[:, half:half + HEAD_DIM]
                t16_ref[a, 0, g, :, l * HEAD_DIM:(l + 1) * HEAD_DIM] = tok.astype(jnp.bfloat16)
    for g in range(N_KV):
        ks_ref[0, g] = _with_features(acc[:, 2 * KVW:3 * KVW], g, key_feat, lane)
        kw_ref[0, g] = _with_features(acc[:, 4 * KVW:5 * KVW], g, key_feat, lane)
        kb_ref[0, g] = _with_features(acc[:, 6 * KVW:7 * KVW], g, key_feat, lane)
    vst = acc[:, 3 * KVW:4 * KVW].T
    vwt = acc[:, 5 * KVW:6 * KVW].T
    vbt = acc[:, 7 * KVW:8 * KVW].T
    ones_rows = jnp.where(_iota((V_ROWS - HEAD_DIM, tm), 0) == 0, 1.0, 0.0).astype(jnp.bfloat16)
    for g in range(N_KV):
        c = g * HEAD_DIM
        for ref, vt in ((vst_ref, vst), (vwt_ref, vwt), (vbt_ref, vbt)):
            ref[0, g] = jnp.concatenate([vt[c:c + HEAD_DIM].astype(jnp.bfloat16), ones_rows], axis=0)

    acc = _dot(hb, w_ref[:, _C_Z:_C_M])
    sz = acc * _sigmoid(acc)
    sza_ref[0] = sz[:, :WIDTH].astype(jnp.bfloat16)
    szb_ref[0] = sz[:, WIDTH:].astype(jnp.bfloat16)

    acc = _dot(hb, w_ref[:, _C_M:_C_M + D_MODEL])
    sma_ref[0] = _sigmoid(acc).astype(jnp.bfloat16)
    acc = _dot(hb, w_ref[:, _C_M + D_MODEL:_C_G])
    smb_ref[0] = _sigmoid(acc).astype(jnp.bfloat16)

    acc = _dot(hb, w_ref[:, _C_G:_C_END])
    gt_ref[0] = _sigmoid(acc).T[:GATE_ROWS]


def _padded_init(B, S):
    pad_row = jnp.zeros((FEAT,), jnp.float32).at[LANE_BLK:LANE_POS].set(PAD_KEY_BLOCK)
    k_init = jnp.broadcast_to(pad_row.astype(jnp.bfloat16), (B, N_KV, KEY_PAD + S, FEAT))
    vt_init = jnp.zeros((B, N_KV, V_ROWS, KEY_PAD + S), jnp.bfloat16)
    return k_init, vt_init


def _in_proj(x, shift, scale, gpre, w):
    B, S, D = x.shape
    tm = TM_PROJ
    assert KEY_PAD == tm and KEY_PAD >= max(NSA_WINDOW, SWA_WINDOW)
    bf = jnp.bfloat16
    q_shape = jax.ShapeDtypeStruct((B, N_KV, GROUP, S, FEAT), bf)
    t16_shape = jax.ShapeDtypeStruct((2, B, N_KV, S // CMP_STRIDE, CMP_STRIDE * HEAD_DIM), bf)
    k_shape = jax.ShapeDtypeStruct((B, N_KV, S, FEAT), bf)
    vt_shape = jax.ShapeDtypeStruct((B, N_KV, V_ROWS, S), bf)
    q_spec = pl.BlockSpec((1, N_KV, GROUP, tm, FEAT), lambda b, i: (b, 0, 0, i, 0))
    t16_spec = pl.BlockSpec((2, 1, N_KV, tm // CMP_STRIDE, CMP_STRIDE * HEAD_DIM),
                            lambda b, i: (0, b, 0, i, 0))
    k_spec = pl.BlockSpec((1, N_KV, tm, FEAT), lambda b, i: (b, 0, i, 0))
    vt_spec = pl.BlockSpec((1, N_KV, V_ROWS, tm), lambda b, i: (b, 0, 0, i))
    kp_shape = jax.ShapeDtypeStruct((B, N_KV, KEY_PAD + S, FEAT), bf)
    vtp_shape = jax.ShapeDtypeStruct((B, N_KV, V_ROWS, KEY_PAD + S), bf)
    kp_spec = pl.BlockSpec((1, N_KV, tm, FEAT), lambda b, i: (b, 0, i + 1, 0))
    vtp_spec = pl.BlockSpec((1, N_KV, V_ROWS, tm), lambda b, i: (b, 0, 0, i + 1))
    k_init, vt_init = _padded_init(B, S)
    any_spec = pl.BlockSpec(memory_space=pl.ANY)
    row_spec = lambda n: pl.BlockSpec((1, tm, n), lambda b, i: (b, i, 0))
    vec_spec = pl.BlockSpec((1, 1, D), lambda b, i: (b, 0, 0))
    return pl.pallas_call(
        _in_proj_kernel,
        out_shape=(q_shape, q_shape, t16_shape, k_shape, vt_shape, kp_shape, vtp_shape,
                   kp_shape, vtp_shape,
                   jax.ShapeDtypeStruct((B, S, WIDTH), bf), jax.ShapeDtypeStruct((B, S, WIDTH), bf),
                   jax.ShapeDtypeStruct((B, S, D), bf), jax.ShapeDtypeStruct((B, S, D), bf),
                   jax.ShapeDtypeStruct((B, GATE_ROWS, S), jnp.float32)),
        grid=(B, S // tm),
        in_specs=[row_spec(D), vec_spec, vec_spec,
                  pl.BlockSpec((1, D), lambda b, i: (0, 0)),
                  pl.BlockSpec((D, _C_END), lambda b, i: (0, 0), pipeline_mode=pl.Buffered(1)),
                  any_spec, any_spec, any_spec, any_spec],
        input_output_aliases={5: 5, 6: 6, 7: 7, 8: 8},
        out_specs=(q_spec, q_spec, t16_spec, k_spec, vt_spec, kp_spec, vtp_spec,
                   kp_spec, vtp_spec,
                   row_spec(WIDTH), row_spec(WIDTH), row_spec(D), row_spec(D),
                   pl.BlockSpec((1, GATE_ROWS, tm), lambda b, i: (b, 0, i))),
        scratch_shapes=[pltpu.VMEM((4, tm, KVW), jnp.float32)],
        compiler_params=pltpu.CompilerParams(
            dimension_semantics=("parallel", "parallel"), vmem_limit_bytes=VMEM_LIMIT),
        name="in_proj",
    )(x, shift, scale, gpre, w, k_init, vt_init, k_init, vt_init)


def _compress_kernel(t_ref, pe_ref, w1_ref, w2_ref, o_ref, ot_ref, w1b_scr, bias_scr):
    half = CMP_STRIDE * HEAD_DIM

    @pl.when(pl.program_id(1) == 0)
    def _():
        w1 = w1_ref[0]
        w1b_scr[...] = w1.astype(jnp.bfloat16)
        pe = jnp.broadcast_to(pe_ref[0], (bias_scr.shape[0], 2 * half))
        bias_scr[...] = jnp.dot(pe, w1, preferred_element_type=jnp.float32,
                                precision=lax.Precision.HIGHEST)

    t = t_ref[0, 0]
    first = _dot(t, w1b_scr[:half])
    second = _dot(t, w1b_scr[half:])
    n = second.shape[0]
    second = pltpu.roll(second, n - 1, 0)
    pre = first + second + bias_scr[0:1]
    hid = pre * _sigmoid(pre)
    out = _dot(hid.astype(jnp.bfloat16), w2_ref[0].astype(jnp.bfloat16))
    lane = _iota((n, FEAT), 1)
    c = _iota((n, FEAT), 0)
    end = c * CMP_STRIDE + (CMP_BLOCK - 1)
    o_ref[0, 0] = jnp.where(lane < HEAD_DIM, out, _key_features(lane, end)).astype(jnp.bfloat16)
    ot_ref[0, 0] = out.T[:HEAD_DIM].astype(jnp.bfloat16)


def _compress(t, pe, w1, w2):
    _, BG, n, wdt = t.shape
    return pl.pallas_call(
        _compress_kernel,
        out_shape=(jax.ShapeDtypeStruct((2, BG, n, FEAT), jnp.bfloat16),
                   jax.ShapeDtypeStruct((2, BG, HEAD_DIM, n), jnp.bfloat16)),
        grid=(2, BG),
        in_specs=[pl.BlockSpec((1, 1, n, wdt), lambda a, i: (a, i, 0, 0)),
                  pl.BlockSpec((1, 1, 2 * wdt), lambda a, i: (a, 0, 0)),
                  pl.BlockSpec((1, 2 * wdt, CMP_HIDDEN), lambda a, i: (a, 0, 0)),
                  pl.BlockSpec((1, CMP_HIDDEN, FEAT), lambda a, i: (a, 0, 0))],
        out_specs=(pl.BlockSpec((1, 1, n, FEAT), lambda a, i: (a, i, 0, 0)),
                   pl.BlockSpec((1, 1, HEAD_DIM, n), lambda a, i: (a, i, 0, 0))),
        scratch_shapes=[pltpu.VMEM((2 * wdt, CMP_HIDDEN), jnp.bfloat16),
                        pltpu.VMEM((8, CMP_HIDDEN), jnp.float32)],
        compiler_params=pltpu.CompilerParams(dimension_semantics=("arbitrary", "arbitrary")),
        name="compress",
    )(t, pe, w1, w2)


def _head_slope(g, r):
    return jnp.where(g == 0, 1.0, 2.0 ** -GROUP).astype(jnp.float32) * (2.0 ** -(r + 1))


def _chunked_max(rows, n_rows):
    m8 = None
    for c in range(0, n_rows, CHUNK):
        chunk = rows(c)
        for r in range(0, CHUNK, 8):
            m8 = chunk[r:r + 8] if m8 is None else jnp.maximum(m8, chunk[r:r + 8])
    return jnp.max(m8, axis=0, keepdims=True)


def _chunked_exp2(rows, n_rows, m):
    return jnp.concatenate(
        [jnp.exp2(rows(c) - m).astype(jnp.bfloat16) for c in range(0, n_rows, CHUNK)], axis=0)


def _per_head(x):
    return jnp.concatenate([x] * GROUP, axis=1)


def _band_scores(q, k_ref, q0, window):
    start = pl.multiple_of(q0 + (KEY_PAD - window), BAND_ALIGN)
    return _dot_nt(k_ref[0, 0, pl.ds(start, window + TQ), :], q)


def _band_finish(st, vt_ref, q0, window, sink_row):
    n_keys = window + TQ
    start = pl.multiple_of(q0 + (KEY_PAD - window), BAND_ALIGN)
    col = _iota((CHUNK, TQ), 1)

    def rows(c):
        part = st[c:c + CHUNK]
        row = c + _iota((CHUNK, TQ), 0)
        if c < TQ:
            part = part + _per_head(jnp.where(row > col, 0.0, NEG))
        if c + CHUNK - 1 > window:
            part = part + _per_head(jnp.where(row <= col + window, 0.0, NEG))
        return part

    m = _chunked_max(rows, n_keys)
    if sink_row is not None:
        m = jnp.maximum(m, sink_row)
    p = _chunked_exp2(rows, n_keys, m)
    o = _dot(vt_ref[0, 0, :, pl.ds(start, n_keys)], p)
    l = o[HEAD_DIM:HEAD_DIM + 1]
    if sink_row is not None:
        l = l + jnp.exp2(sink_row - m)
    return o[:HEAD_DIM] / l


def _store_heads(o_t, sz_ref, o_ref):
    stacked = jnp.concatenate([o_t[:, r * TQ:(r + 1) * TQ] for r in range(GROUP)], axis=0)
    o_ref[0] = (stacked.T * sz_ref[0].astype(jnp.float32)).astype(jnp.bfloat16)


def _importance_matrix(nsb, n_cmp):
    ratio = SLC_BLOCK // CMP_STRIDE
    span = CMP_BLOCK // CMP_STRIDE
    off = ratio * np.arange(nsb)[:, None] - np.arange(n_cmp)[None, :]
    w = sum((off == mm + nn).astype(np.float32) for mm in range(ratio) for nn in range(span))
    w[:, n_cmp - 1] = 0.0
    return jnp.asarray(w, jnp.bfloat16)


def _selection_rank(score):
    nsb = score.shape[0]
    groups = [score[8 * a:8 * a + 8] for a in range(nsb // 8)]
    sub = _iota((8, TQ), 0)
    rank = [jnp.zeros((8, TQ), jnp.float32) for _ in groups]
    for i in range(nsb):
        row = jnp.broadcast_to(score[i:i + 1, :], (8, TQ))
        for a, grp in enumerate(groups):
            if a < i // 8:
                ahead = jnp.where(row > grp, 1.0, 0.0)
            elif a > i // 8:
                ahead = jnp.where(row >= grp, 1.0, 0.0)
            else:
                ahead = jnp.where(sub > i % 8, jnp.where(row >= grp, 1.0, 0.0),
                                  jnp.where(row > grp, 1.0, 0.0))
            rank[a] = rank[a] + ahead
    return jnp.concatenate(rank, axis=0)


def _mixers_kernel(sinks_ref, qa_ref, qb_ref, kc_ref, vct_ref, ks_ref, vst_ref, kw_ref, vwt_ref,
                   kb_ref, vbt_ref, onehot_ref, wimp_ref, gt_ref, sza_ref, szb_ref, oa_ref, ob_ref,
                   s_buf):
    g = pl.program_id(1)
    qi = pl.program_id(2)
    q0 = qi * TQ
    q = qa_ref[0, 0].reshape(GROUP * TQ, FEAT)

    n_cmp = kc_ref.shape[1]
    st_cmp = _dot_nt(kc_ref[0], q)
    st_swa = _band_scores(qb_ref[0, 0].reshape(GROUP * TQ, FEAT), kb_ref, q0, SWA_WINDOW)
    st_win = _band_scores(q, kw_ref, q0, NSA_WINDOW)

    qpos_c = q0 + _iota((n_cmp, TQ), 1)
    end_c = _iota((n_cmp, TQ), 0) * CMP_STRIDE + (CMP_BLOCK - 1)
    hidden = jnp.where(qpos_c >= end_c, 0.0, NEG)
    any_valid = (qpos_c[0:1] >= CMP_BLOCK - 1).astype(jnp.float32)
    p_sum = jnp.zeros((n_cmp, TQ), jnp.float32)
    o_cmp = []
    for r in range(GROUP):
        sr = st_cmp[:, r * TQ:(r + 1) * TQ] + hidden
        er = jnp.exp2(sr - jnp.max(sr, axis=0, keepdims=True))
        inv = any_valid / jnp.sum(er, axis=0, keepdims=True)
        p_sum = p_sum + er * inv
        o_cmp.append(_dot(vct_ref[0], er.astype(jnp.bfloat16)) * inv)
    o_cmp = jnp.concatenate(o_cmp, axis=1)

    wimp = wimp_ref[...]
    p_hi = p_sum.astype(jnp.bfloat16)
    rem = p_sum - p_hi.astype(jnp.float32)
    p_mid = rem.astype(jnp.bfloat16)
    p_lo = (rem - p_mid.astype(jnp.float32)).astype(jnp.bfloat16)
    imp = _dot(wimp, p_hi) + _dot(wimp, p_mid) + _dot(wimp, p_lo)

    qpos = (q0 + _iota((1, TQ), 1)).astype(jnp.float32)
    qpos_log2 = sum(qpos * piece for piece in LOG2E_PIECES)
    sink_row = jnp.concatenate(
        [sinks_ref[g * GROUP + r] * LOG2E + _head_slope(g, r) * qpos_log2 for r in range(GROUP)],
        axis=1)
    _store_heads(_band_finish(st_swa, vbt_ref, q0, SWA_WINDOW, sink_row), szb_ref, ob_ref)

    o_win = _band_finish(st_win, vwt_ref, q0, NSA_WINDOW, None)

    nsb = imp.shape[0]
    jrow = _iota((nsb, TQ), 0)
    cur = lax.shift_right_logical(q0 + _iota((nsb, TQ), 1), 6)
    causal = jrow <= cur
    forced = (jrow == 0) | (jrow == cur) | (jrow == cur - 1)
    score = jnp.where(causal, jnp.where(forced, BIG, imp), NEG)
    rank = _selection_rank(score)
    unselected = jnp.where(causal, jnp.where(rank < float(N_SELECT), 0.0, NEG_FEATURE), NEG_FEATURE)
    mask_feat = jnp.concatenate([unselected, jnp.zeros((FEAT - nsb, TQ), jnp.float32)], axis=0).T
    q_sel = jnp.concatenate(
        [q, jnp.concatenate([mask_feat.astype(jnp.bfloat16)] * GROUP, axis=0)], axis=1)

    tk = TK_SLC
    diag_pair = (q0 + TQ - 1) // (2 * tk)

    def slc_scores(j):
        k0 = pl.multiple_of(j * tk, tk)
        k_aug = jnp.concatenate([ks_ref[0, 0, pl.ds(k0, tk), :], onehot_ref[pl.ds(k0, tk), :]], axis=1)
        return _dot_nt(k_aug, q_sel)

    def slc_update(j, slot, carry, near_diagonal):
        m, acc = carry
        k0 = pl.multiple_of(j * tk, tk)

        def rows(c):
            s = s_buf[slot, c:c + CHUNK, :]
            if near_diagonal:
                dist = (q0 + _iota((CHUNK, TQ), 1)) - (k0 + c + _iota((CHUNK, TQ), 0))
                s = s + _per_head(jnp.where(dist >= 0, 0.0, NEG))
            return s

        m_new = jnp.maximum(m, _chunked_max(rows, tk))
        alpha = jnp.exp2(m - m_new)
        p = _chunked_exp2(rows, tk, m_new)
        acc = alpha * acc + _dot(vst_ref[0, 0, :, pl.ds(k0, tk)], p)
        return m_new, acc

    s_buf[2] = slc_scores(2 * diag_pair)
    s_buf[3] = slc_scores(2 * diag_pair + 1)
    s_buf[0] = slc_scores(0)

    n = GROUP * TQ
    carry = (jnp.full((1, n), M_FLOOR, jnp.float32), jnp.zeros((V_ROWS, n), jnp.float32))
    carry = slc_update(2 * diag_pair, 2, carry, True)
    carry = slc_update(2 * diag_pair + 1, 3, carry, True)

    def slc_pair(jj, carry):
        j = 2 * jj
        s_buf[1] = slc_scores(j + 1)
        carry = slc_update(j, 0, carry, False)
        s_buf[0] = slc_scores(j + 2)
        return slc_update(j + 1, 1, carry, False)

    _, acc = lax.fori_loop(0, diag_pair, slc_pair, carry)
    o_slc = acc[:HEAD_DIM] / acc[HEAD_DIM:HEAD_DIM + 1]

    pieces = []
    for r in range(GROUP):
        sl = slice(r * TQ, (r + 1) * TQ)
        gate = [gt_ref[0, pl.ds(br * N_HEADS + g * GROUP + r, 1), :] for br in range(3)]
        pieces.append(gate[0] * o_cmp[:, sl] + gate[1] * o_slc[:, sl] + gate[2] * o_win[:, sl])
    _store_heads(jnp.concatenate(pieces, axis=1), sza_ref, oa_ref)


def _mixers(sinks, qa, qb, kc, vct, ks, vst, kw, vwt, kb, vbt, onehot, wimp, gt, sza, szb):
    B, _, _, S, _ = qa.shape
    n_cmp = kc.shape[1]
    q_spec = pl.BlockSpec((1, 1, GROUP, TQ, FEAT), lambda b, g, i: (b, g, 0, i, 0))
    seq_k = lambda n: pl.BlockSpec((1, 1, n, FEAT), lambda b, g, i: (b, g, 0, 0))
    seq_vt = lambda n: pl.BlockSpec((1, 1, V_ROWS, n), lambda b, g, i: (b, g, 0, 0))
    o_spec = pl.BlockSpec((1, TQ, GROUP * HEAD_DIM), lambda b, g, i: (b, i, g))
    o_shape = jax.ShapeDtypeStruct((B, S, WIDTH), jnp.bfloat16)
    return pl.pallas_call(
        _mixers_kernel,
        out_shape=(o_shape, o_shape),
        grid=(B, N_KV, S // TQ),
        in_specs=[pl.BlockSpec(memory_space=pltpu.SMEM), q_spec, q_spec,
                  pl.BlockSpec((1, n_cmp, FEAT), lambda b, g, i: (b * N_KV + g, 0, 0)),
                  pl.BlockSpec((1, HEAD_DIM, n_cmp), lambda b, g, i: (b * N_KV + g, 0, 0)),
                  seq_k(S), seq_vt(S), seq_k(kw.shape[2]), seq_vt(kw.shape[2]),
                  seq_k(kb.shape[2]), seq_vt(kb.shape[2]),
                  pl.BlockSpec((S, FEAT), lambda b, g, i: (0, 0)),
                  pl.BlockSpec(wimp.shape, lambda b, g, i: (0, 0)),
                  pl.BlockSpec((1, GATE_ROWS, TQ), lambda b, g, i: (b, 0, i)),
                  o_spec, o_spec],
        out_specs=(o_spec, o_spec),
        scratch_shapes=[pltpu.VMEM((4, TK_SLC, GROUP * TQ), jnp.float32)],
        compiler_params=pltpu.CompilerParams(
            dimension_semantics=("parallel", "parallel", "arbitrary"),
            vmem_limit_bytes=VMEM_LIMIT),
        name="mixers",
    )(sinks, qa, qb, kc, vct, ks, vst, kw, vwt, kb, vbt, onehot, wimp, gt, sza, szb)


def _out_proj_kernel(x_ref, oza_ref, ozb_ref, sma_ref, smb_ref, gate_ref, gpost_ref,
                     woa_ref, wob_ref, wout_ref, o_ref):
    ya = _dot(oza_ref[0], woa_ref[...])
    yb = _dot(ozb_ref[0], wob_ref[...])
    y = sma_ref[0].astype(jnp.float32) * ya + smb_ref[0].astype(jnp.float32) * yb
    yo = _dot(y.astype(jnp.bfloat16), wout_ref[...])
    ms = jnp.mean(yo * yo, axis=-1, keepdims=True)
    normed = yo * lax.rsqrt(ms + RMS_EPS) * gpost_ref[...]
    o_ref[0] = x_ref[0] + gate_ref[0] * normed


def _out_proj(x, oza, ozb, sma, smb, gate, gpost, woa, wob, wout):
    B, S, D = x.shape
    tm = TM_PROJ
    row_spec = lambda n: pl.BlockSpec((1, tm, n), lambda b, i: (b, i, 0))
    const = lambda shape: pl.BlockSpec(shape, lambda b, i: (0,) * len(shape))
    return pl.pallas_call(
        _out_proj_kernel,
        out_shape=jax.ShapeDtypeStruct((B, S, D), jnp.float32),
        grid=(B, S // tm),
        in_specs=[row_spec(D), row_spec(WIDTH), row_spec(WIDTH), row_spec(D), row_spec(D),
                  pl.BlockSpec((1, 1, D), lambda b, i: (b, 0, 0)), const((1, D)),
                  const((WIDTH, D)), const((WIDTH, D)), const((D, D))],
        out_specs=row_spec(D),
        compiler_params=pltpu.CompilerParams(
            dimension_semantics=("parallel", "parallel"), vmem_limit_bytes=VMEM_LIMIT),
        name="out_proj",
    )(x, oza, ozb, sma, smb, gate, gpost, woa, wob, wout)


def _arrange_w_in(w):
    o = 0
    seg = {}
    for name, n in (("qa", WIDTH), ("kvc", 2 * KVW), ("kvs", 2 * KVW), ("kvw", 2 * KVW),
                    ("g", N_GATES), ("za", WIDTH), ("qb", WIDTH), ("kvb", 2 * KVW),
                    ("zb", WIDTH), ("m", 2 * D_MODEL)):
        seg[name] = w[:, o:o + n]
        o += n
    scale = HEAD_DIM ** -0.5 * LOG2E
    gpad = jnp.pad(seg["g"], ((0, 0), (0, GATE_PAD - N_GATES)))
    cols = [seg["qa"] * scale, seg["qb"] * scale, seg["kvc"], seg["kvs"], seg["kvw"], seg["kvb"],
            seg["za"], seg["zb"], seg["m"], gpad]
    return jnp.concatenate(cols, axis=1).astype(jnp.bfloat16)


def _block_onehot(S):
    blk = jnp.arange(S, dtype=jnp.int32)[:, None] // SLC_BLOCK
    return (blk == jnp.arange(FEAT, dtype=jnp.int32)[None, :]).astype(jnp.bfloat16)


def kernel(x, c, w_ada, b_ada, g_pre, g_post, w_in, pe_cmp_k, pe_cmp_v, w_cmp_k1, w_cmp_k2,
           w_cmp_v1, w_cmp_v2, w_o_nsa, w_o_swa, w_out, sinks):
    B, S, D = x.shape
    depth = w_in.shape[0]
    bf = jnp.bfloat16
    for l in range(depth):
        c8 = jnp.pad(c, ((0, 8 - B), (0, 0)))
        mod = _adaln_mod(c8, w_ada[l], b_ada[l][None, :])[:B]
        shift, scale, gate = (mod[:, None, i * D:(i + 1) * D] for i in range(3))

        (qa, qb, t16, ks, vst, kw, vwt, kb, vbt, sza, szb, sma, smb, gt) = _in_proj(
            x, shift, scale, g_pre[l][None, :], _arrange_w_in(w_in[l]))

        rows = S // CMP_STRIDE
        t = t16.reshape(2, B * N_KV, rows, CMP_STRIDE * HEAD_DIM)
        pe = jnp.stack([pe_cmp_k[l], pe_cmp_v[l]]).reshape(2, 1, CMP_BLOCK * HEAD_DIM)
        w2 = jnp.pad(jnp.stack([w_cmp_k2[l], w_cmp_v2[l]]), ((0, 0), (0, 0), (0, FEAT - HEAD_DIM)))
        cmp_o, cmp_ot = _compress(t, pe, jnp.stack([w_cmp_k1[l], w_cmp_v1[l]]), w2)

        oza, ozb = _mixers(sinks[l], qa, qb, cmp_o[0], cmp_ot[1], ks, vst, kw, vwt, kb, vbt,
                           _block_onehot(S), _importance_matrix(S // SLC_BLOCK, rows), gt, sza, szb)

        x = _out_proj(x, oza, ozb, sma, smb, gate, g_post[l][None, :],
                      w_o_nsa[l].astype(bf), w_o_swa[l].astype(bf), w_out[l].astype(bf))
    return x
```

```python
import jax
import jax.numpy as jnp
import numpy as np
from jax import lax
from jax.experimental import pallas as pl
from jax.experimental.pallas import tpu as pltpu

D_MODEL = 1024
HEAD_DIM = 64
N_HEADS = 8
N_KV = 2
GROUP = N_HEADS // N_KV
CMP_BLOCK = 32
CMP_STRIDE = 16
CMP_HIDDEN = 256
SLC_BLOCK = 64
N_SELECT = 16
NSA_WINDOW = 512
SWA_WINDOW = 128
RMS_EPS = 1e-6
NEG = -1e30
BIG = 1e30
M_FLOOR = -1e20
NEG_FEATURE = -(2.0 ** 100)
PAD_KEY_BLOCK = -(2.0 ** 90)

WIDTH = N_HEADS * HEAD_DIM
KVW = N_KV * HEAD_DIM
FEAT = 128
V_ROWS = 80


def _bf16_pieces(x, n):
    out = []
    for _ in range(n):
        bits = np.array(x, np.float32).view(np.uint32)
        bits = (bits + np.uint32(0x7FFF) + ((bits >> np.uint32(16)) & np.uint32(1))) & np.uint32(0xFFFF0000)
        piece = float(bits.view(np.float32))
        out.append(piece)
        x -= piece
    return tuple(out)


LOG2E_PIECES = _bf16_pieces(1.4426950408889634, 3)
LOG2E = sum(LOG2E_PIECES)
N_PIECES = len(LOG2E_PIECES)
LANE_BLK = HEAD_DIM
LANE_POS = HEAD_DIM + N_PIECES
N_GATES = 3 * N_HEADS
GATE_PAD = 128
GATE_ROWS = 32

VMEM_LIMIT = 56 * 1024 * 1024

TM_PROJ = 512
TQ = 256
TK_SLC = 512
CHUNK = 16
KEY_PAD = TM_PROJ
BAND_ALIGN = 128

_NT = (((1,), (1,)), ((), ()))


def _dot(a, b):
    return jnp.dot(a, b, preferred_element_type=jnp.float32)


def _dot_nt(a, b):
    return lax.dot_general(a, b, _NT, preferred_element_type=jnp.float32)


def _sigmoid(v):
    return 1.0 / (1.0 + jnp.exp(-v))


def _iota(shape, dim):
    return lax.broadcasted_iota(jnp.int32, shape, dim)


def _key_features(lane, pos):
    blk = lax.shift_right_logical(pos, 6).astype(jnp.float32)
    within = (pos & (SLC_BLOCK - 1)).astype(jnp.float32)
    return jnp.where(lane < LANE_POS, blk, jnp.where(lane < LANE_POS + N_PIECES, within, 0.0))


def _query_features(lane, slope):
    feat = jnp.zeros(lane.shape, jnp.float32)
    for i, piece in enumerate(LOG2E_PIECES):
        feat = jnp.where(lane == LANE_BLK + i, slope * SLC_BLOCK * piece, feat)
        feat = jnp.where(lane == LANE_POS + i, slope * piece, feat)
    return feat


def _adaln_kernel(c_ref, w_ref, b_ref, o_ref):
    o_ref[...] = jnp.dot(c_ref[...], w_ref[...], preferred_element_type=jnp.float32,
                         precision=lax.Precision.HIGHEST) + b_ref[...]


def _adaln_mod(c8, w, b):
    n = w.shape[1]
    bn = 1024
    return pl.pallas_call(
        _adaln_kernel,
        out_shape=jax.ShapeDtypeStruct((c8.shape[0], n), jnp.float32),
        grid=(n // bn,),
        in_specs=[pl.BlockSpec((c8.shape[0], D_MODEL), lambda j: (0, 0)),
                  pl.BlockSpec((D_MODEL, bn), lambda j: (0, j)),
                  pl.BlockSpec((1, bn), lambda j: (0, j))],
        out_specs=pl.BlockSpec((c8.shape[0], bn), lambda j: (0, j)),
        name="adaln_mod",
    )(c8, w, b)


_C_QA = 0
_C_KVA = _C_QA + WIDTH
_C_G = _C_KVA + 6 * KVW
_C_ZA = _C_G + GATE_PAD
_C_QB = _C_ZA + WIDTH
_C_KVB = _C_QB + WIDTH
_C_ZB = _C_KVB + 2 * KVW
_C_M = _C_ZB + WIDTH
_C_END = _C_M + 2 * D_MODEL


def _with_features(pair, odd, feat, lane):
    src = pltpu.roll(pair, HEAD_DIM, 1) if odd else pair
    return jnp.where(lane < HEAD_DIM, src, feat).astype(jnp.bfloat16)


def _in_proj_kernel(x_ref, shift_ref, scale_ref, gpre_ref, w_ref, _kw0, _vwt0, _kb0, _vbt0,
                    qa_ref, qb_ref, t16_ref, ks_ref, vst_ref, kw_ref, vwt_ref,
                    kb_ref, vbt_ref, sza_ref, szb_ref, sma_ref, smb_ref, gt_ref, kv_scr):
    tm = x_ref.shape[1]
    xf = x_ref[0]
    ms = jnp.mean(xf * xf, axis=-1, keepdims=True)
    y = xf * lax.rsqrt(ms + RMS_EPS) * gpre_ref[...]
    h = y * (1.0 + scale_ref[0]) + shift_ref[0]
    hb = h.astype(jnp.bfloat16)

    lane = _iota((tm, FEAT), 1)
    pos = pl.program_id(1) * tm + _iota((tm, FEAT), 0)
    key_feat = _key_features(lane, pos)

    def store_queries(acc, q_ref):
        for head in range(N_HEADS):
            q_feat = _query_features(lane, 2.0 ** -(head + 1))
            g, r, odd = head // GROUP, head % GROUP, head % 2
            c = (head // 2) * FEAT
            q_ref[0, g, r] = _with_features(acc[:, c:c + FEAT], odd, q_feat, lane)

    def store_values(acc, c0, vt_ref):
        vt = acc[:, c0:c0 + KVW].T
        ones_rows = jnp.where(_iota((V_ROWS - HEAD_DIM, tm), 0) == 0, 1.0, 0.0).astype(jnp.bfloat16)
        for g in range(N_KV):
            rows = vt[g * HEAD_DIM:(g + 1) * HEAD_DIM].astype(jnp.bfloat16)
            vt_ref[0, g] = jnp.concatenate([rows, ones_rows], axis=0)

    acc = _dot(hb, w_ref[:, _C_QA:_C_G])
    store_queries(acc, qa_ref)
    acc = acc[:, _C_KVA:]
    for a in range(2):
        kv_scr[a] = acc[:, a * KVW:(a + 1) * KVW]
        kv_scr[2 + a] = pltpu.roll(acc[:, a * KVW:(a + 1) * KVW], HEAD_DIM, 1)
    for l in range(CMP_STRIDE):
        for a in range(2):
            plain = kv_scr[a, pl.ds(l, tm // CMP_STRIDE, stride=CMP_STRIDE), :]
            rolled = kv_scr[2 + a, pl.ds(l, tm // CMP_STRIDE, stride=CMP_STRIDE), :]
            half = (l % 2) * HEAD_DIM
            for g in range(N_KV):
                tok = (plain if g % 2 == l % 2 else rolled)[:, half:half + HEAD_DIM]
                t16_ref[a, 0, g, :, l * HEAD_DIM:(l + 1) * HEAD_DIM] = tok.astype(jnp.bfloat16)
    for g in range(N_KV):
        ks_ref[0, g] = _with_features(acc[:, 2 * KVW:3 * KVW], g, key_feat, lane)
        kw_ref[0, g] = _with_features(acc[:, 4 * KVW:5 * KVW], g, key_feat, lane)
    store_values(acc, 3 * KVW, vst_ref)
    store_values(acc, 5 * KVW, vwt_ref)

    acc = _dot(hb, w_ref[:, _C_G:_C_ZA])
    gt_ref[0] = _sigmoid(acc).T[:GATE_ROWS]

    acc = _dot(hb, w_ref[:, _C_ZA:_C_M])
    za = acc[:, :WIDTH]
    sza_ref[0] = (za * _sigmoid(za)).astype(jnp.bfloat16)
    store_queries(acc[:, _C_QB - _C_ZA:_C_KVB - _C_ZA], qb_ref)
    kvb = acc[:, _C_KVB - _C_ZA:_C_ZB - _C_ZA]
    for g in range(N_KV):
        kb_ref[0, g] = _with_features(kvb[:, :KVW], g, key_feat, lane)
    store_values(kvb, KVW, vbt_ref)
    zb = acc[:, _C_ZB - _C_ZA:]
    szb_ref[0] = (zb * _sigmoid(zb)).astype(jnp.bfloat16)

    acc = _dot(hb, w_ref[:, _C_M:_C_M + D_MODEL])
    sma_ref[0] = _sigmoid(acc).astype(jnp.bfloat16)
    acc = _dot(hb, w_ref[:, _C_M + D_MODEL:_C_END])
    smb_ref[0] = _sigmoid(acc).astype(jnp.bfloat16)


def _padded_init(B, S):
    pad_row = jnp.zeros((FEAT,), jnp.float32).at[LANE_BLK:LANE_POS].set(PAD_KEY_BLOCK)
    k_init = jnp.broadcast_to(pad_row.astype(jnp.bfloat16), (B, N_KV, KEY_PAD + S, FEAT))
    vt_init = jnp.zeros((B, N_KV, V_ROWS, KEY_PAD + S), jnp.bfloat16)
    return k_init, vt_init


def _in_proj(x, shift, scale, gpre, w):
    B, S, D = x.shape
    tm = TM_PROJ
    assert KEY_PAD == tm and KEY_PAD >= max(NSA_WINDOW, SWA_WINDOW)
    bf = jnp.bfloat16
    q_shape = jax.ShapeDtypeStruct((B, N_KV, GROUP, S, FEAT), bf)
    t16_shape = jax.ShapeDtypeStruct((2, B, N_KV, S // CMP_STRIDE, CMP_STRIDE * HEAD_DIM), bf)
    k_shape = jax.ShapeDtypeStruct((B, N_KV, S, FEAT), bf)
    vt_shape = jax.ShapeDtypeStruct((B, N_KV, V_ROWS, S), bf)
    q_spec = pl.BlockSpec((1, N_KV, GROUP, tm, FEAT), lambda b, i: (b, 0, 0, i, 0))
    t16_spec = pl.BlockSpec((2, 1, N_KV, tm // CMP_STRIDE, CMP_STRIDE * HEAD_DIM),
                            lambda b, i: (0, b, 0, i, 0))
    k_spec = pl.BlockSpec((1, N_KV, tm, FEAT), lambda b, i: (b, 0, i, 0))
    vt_spec = pl.BlockSpec((1, N_KV, V_ROWS, tm), lambda b, i: (b, 0, 0, i))
    kp_shape = jax.ShapeDtypeStruct((B, N_KV, KEY_PAD + S, FEAT), bf)
    vtp_shape = jax.ShapeDtypeStruct((B, N_KV, V_ROWS, KEY_PAD + S), bf)
    kp_spec = pl.BlockSpec((1, N_KV, tm, FEAT), lambda b, i: (b, 0, i + 1, 0))
    vtp_spec = pl.BlockSpec((1, N_KV, V_ROWS, tm), lambda b, i: (b, 0, 0, i + 1))
    k_init, vt_init = _padded_init(B, S)
    any_spec = pl.BlockSpec(memory_space=pl.ANY)
    row_spec = lambda n: pl.BlockSpec((1, tm, n), lambda b, i: (b, i, 0))
    vec_spec = pl.BlockSpec((1, 1, D), lambda b, i: (b, 0, 0))
    return pl.pallas_call(
        _in_proj_kernel,
        out_shape=(q_shape, q_shape, t16_shape, k_shape, vt_shape, kp_shape, vtp_shape,
                   kp_shape, vtp_shape,
                   jax.ShapeDtypeStruct((B, S, WIDTH), bf), jax.ShapeDtypeStruct((B, S, WIDTH), bf),
                   jax.ShapeDtypeStruct((B, S, D), bf), jax.ShapeDtypeStruct((B, S, D), bf),
                   jax.ShapeDtypeStruct((B, GATE_ROWS, S), jnp.float32)),
        grid=(B, S // tm),
        in_specs=[row_spec(D), vec_spec, vec_spec,
                  pl.BlockSpec((1, D), lambda b, i: (0, 0)),
                  pl.BlockSpec((D, _C_END), lambda b, i: (0, 0), pipeline_mode=pl.Buffered(1)),
                  any_spec, any_spec, any_spec, any_spec],
        input_output_aliases={5: 5, 6: 6, 7: 7, 8: 8},
        out_specs=(q_spec, q_spec, t16_spec, k_spec, vt_spec, kp_spec, vtp_spec,
                   kp_spec, vtp_spec,
                   row_spec(WIDTH), row_spec(WIDTH), row_spec(D), row_spec(D),
                   pl.BlockSpec((1, GATE_ROWS, tm), lambda b, i: (b, 0, i))),
        scratch_shapes=[pltpu.VMEM((4, tm, KVW), jnp.float32)],
        compiler_params=pltpu.CompilerParams(
            dimension_semantics=("parallel", "parallel"), vmem_limit_bytes=VMEM_LIMIT),
        name="in_proj",
    )(x, shift, scale, gpre, w, k_init, vt_init, k_init, vt_init)


def _compress_kernel(t_ref, pe_ref, w1_ref, w2_ref, o_ref, ot_ref, w1b_scr, bias_scr):
    half = CMP_STRIDE * HEAD_DIM

    @pl.when(pl.program_id(1) == 0)
    def _():
        w1 = w1_ref[0]
        w1b_scr[...] = w1.astype(jnp.bfloat16)
        pe = jnp.broadcast_to(pe_ref[0], (bias_scr.shape[0], 2 * half))
        bias_scr[...] = jnp.dot(pe, w1, preferred_element_type=jnp.float32,
                                precision=lax.Precision.HIGHEST)

    t = t_ref[0, 0]
    first = _dot(t, w1b_scr[:half])
    second = _dot(t, w1b_scr[half:])
    n = second.shape[0]
    second = pltpu.roll(second, n - 1, 0)
    pre = first + second + bias_scr[0:1]
    hid = pre * _sigmoid(pre)
    out = _dot(hid.astype(jnp.bfloat16), w2_ref[0].astype(jnp.bfloat16))
    lane = _iota((n, FEAT), 1)
    c = _iota((n, FEAT), 0)
    end = c * CMP_STRIDE + (CMP_BLOCK - 1)
    o_ref[0, 0] = jnp.where(lane < HEAD_DIM, out, _key_features(lane, end)).astype(jnp.bfloat16)
    ot_ref[0, 0] = out.T[:HEAD_DIM].astype(jnp.bfloat16)


def _compress(t, pe, w1, w2):
    _, BG, n, wdt = t.shape
    return pl.pallas_call(
        _compress_kernel,
        out_shape=(jax.ShapeDtypeStruct((2, BG, n, FEAT), jnp.bfloat16),
                   jax.ShapeDtypeStruct((2, BG, HEAD_DIM, n), jnp.bfloat16)),
        grid=(2, BG),
        in_specs=[pl.BlockSpec((1, 1, n, wdt), lambda a, i: (a, i, 0, 0)),
                  pl.BlockSpec((1, 1, 2 * wdt), lambda a, i: (a, 0, 0)),
                  pl.BlockSpec((1, 2 * wdt, CMP_HIDDEN), lambda a, i: (a, 0, 0)),
                  pl.BlockSpec((1, CMP_HIDDEN, FEAT), lambda a, i: (a, 0, 0))],
        out_specs=(pl.BlockSpec((1, 1, n, FEAT), lambda a, i: (a, i, 0, 0)),
                   pl.BlockSpec((1, 1, HEAD_DIM, n), lambda a, i: (a, i, 0, 0))),
        scratch_shapes=[pltpu.VMEM((2 * wdt, CMP_HIDDEN), jnp.bfloat16),
                        pltpu.VMEM((8, CMP_HIDDEN), jnp.float32)],
        compiler_params=pltpu.CompilerParams(dimension_semantics=("arbitrary", "arbitrary")),
        name="compress",
    )(t, pe, w1, w2)


def _head_slope(g, r):
    return jnp.where(g == 0, 1.0, 2.0 ** -GROUP).astype(jnp.float32) * (2.0 ** -(r + 1))


def _chunked_max(rows, n_rows):
    m8 = None
    for c in range(0, n_rows, CHUNK):
        chunk = rows(c)
        for r in range(0, CHUNK, 8):
            m8 = chunk[r:r + 8] if m8 is None else jnp.maximum(m8, chunk[r:r + 8])
    return jnp.max(m8, axis=0, keepdims=True)


def _chunked_exp2(rows, n_rows, m):
    return jnp.concatenate(
        [jnp.exp2(rows(c) - m).astype(jnp.bfloat16) for c in range(0, n_rows, CHUNK)], axis=0)


def _per_head(x):
    return jnp.concatenate([x] * GROUP, axis=1)


def _band_scores(q, k_ref, q0, window):
    start = pl.multiple_of(q0 + (KEY_PAD - window), BAND_ALIGN)
    return _dot_nt(k_ref[0, 0, pl.ds(start, window + TQ), :], q)


def _band_finish(st, vt_ref, q0, window, sink_row):
    n_keys = window + TQ
    start = pl.multiple_of(q0 + (KEY_PAD - window), BAND_ALIGN)
    col = _iota((CHUNK, TQ), 1)

    def rows(c):
        part = st[c:c + CHUNK]
        row = c + _iota((CHUNK, TQ), 0)
        if c < TQ:
            part = part + _per_head(jnp.where(row > col, 0.0, NEG))
        if c + CHUNK - 1 > window:
            part = part + _per_head(jnp.where(row <= col + window, 0.0, NEG))
        return part

    m = _chunked_max(rows, n_keys)
    if sink_row is not None:
        m = jnp.maximum(m, sink_row)
    p = _chunked_exp2(rows, n_keys, m)
    o = _dot(vt_ref[0, 0, :, pl.ds(start, n_keys)], p)
    l = o[HEAD_DIM:HEAD_DIM + 1]
    if sink_row is not None:
        l = l + jnp.exp2(sink_row - m)
    return o[:HEAD_DIM] / l


def _store_heads(o_t, sz_ref, o_ref):
    stacked = jnp.concatenate([o_t[:, r * TQ:(r + 1) * TQ] for r in range(GROUP)], axis=0)
    o_ref[0] = (stacked.T * sz_ref[0].astype(jnp.float32)).astype(jnp.bfloat16)


def _importance_matrix(nsb, n_cmp):
    ratio = SLC_BLOCK // CMP_STRIDE
    span = CMP_BLOCK // CMP_STRIDE
    off = ratio * np.arange(nsb)[:, None] - np.arange(n_cmp)[None, :]
    w = sum((off == mm + nn).astype(np.float32) for mm in range(ratio) for nn in range(span))
    w[:, n_cmp - 1] = 0.0
    return jnp.asarray(w, jnp.bfloat16)


def _selection_rank(score):
    nsb = score.shape[0]
    groups = [score[8 * a:8 * a + 8] for a in range(nsb // 8)]
    sub = _iota((8, TQ), 0)
    rank = [jnp.zeros((8, TQ), jnp.float32) for _ in groups]
    for i in range(nsb):
        row = jnp.broadcast_to(score[i:i + 1, :], (8, TQ))
        for a, grp in enumerate(groups):
            if a < i // 8:
                ahead = jnp.where(row > grp, 1.0, 0.0)
            elif a > i // 8:
                ahead = jnp.where(row >= grp, 1.0, 0.0)
            else:
                ahead = jnp.where(sub > i % 8, jnp.where(row >= grp, 1.0, 0.0),
                                  jnp.where(row > grp, 1.0, 0.0))
            rank[a] = rank[a] + ahead
    return jnp.concatenate(rank, axis=0)


def _mixers_kernel(sinks_ref, qa_ref, qb_ref, kc_ref, vct_ref, ks_ref, vst_ref, kw_ref, vwt_ref,
                   kb_ref, vbt_ref, onehot_ref, wimp_ref, gt_ref, sza_ref, szb_ref, oa_ref, ob_ref,
                   s_buf):
    g = pl.program_id(1)
    qi = pl.program_id(2)
    q0 = qi * TQ
    q = qa_ref[0, 0].reshape(GROUP * TQ, FEAT)

    n_cmp = kc_ref.shape[1]
    st_cmp = _dot_nt(kc_ref[0], q)
    st_swa = _band_scores(qb_ref[0, 0].reshape(GROUP * TQ, FEAT), kb_ref, q0, SWA_WINDOW)
    st_win = _band_scores(q, kw_ref, q0, NSA_WINDOW)

    qpos_c = q0 + _iota((n_cmp, TQ), 1)
    end_c = _iota((n_cmp, TQ), 0) * CMP_STRIDE + (CMP_BLOCK - 1)
    hidden = jnp.where(qpos_c >= end_c, 0.0, NEG)
    any_valid = (qpos_c[0:1] >= CMP_BLOCK - 1).astype(jnp.float32)
    p_sum = jnp.zeros((n_cmp, TQ), jnp.float32)
    o_cmp = []
    for r in range(GROUP):
        sr = st_cmp[:, r * TQ:(r + 1) * TQ] + hidden
        er = jnp.exp2(sr - jnp.max(sr, axis=0, keepdims=True))
        inv = any_valid / jnp.sum(er, axis=0, keepdims=True)
        p_sum = p_sum + er * inv
        o_cmp.append(_dot(vct_ref[0], er.astype(jnp.bfloat16)) * inv)
    o_cmp = jnp.concatenate(o_cmp, axis=1)

    wimp = wimp_ref[...]
    p_hi = p_sum.astype(jnp.bfloat16)
    rem = p_sum - p_hi.astype(jnp.float32)
    p_mid = rem.astype(jnp.bfloat16)
    p_lo = (rem - p_mid.astype(jnp.float32)).astype(jnp.bfloat16)
    imp = _dot(wimp, p_hi) + _dot(wimp, p_mid) + _dot(wimp, p_lo)

    qpos = (q0 + _iota((1, TQ), 1)).astype(jnp.float32)
    qpos_log2 = sum(qpos * piece for piece in LOG2E_PIECES)
    sink_row = jnp.concatenate(
        [sinks_ref[g * GROUP + r] * LOG2E + _head_slope(g, r) * qpos_log2 for r in range(GROUP)],
        axis=1)
    _store_heads(_band_finish(st_swa, vbt_ref, q0, SWA_WINDOW, sink_row), szb_ref, ob_ref)

    o_win = _band_finish(st_win, vwt_ref, q0, NSA_WINDOW, None)

    nsb = imp.shape[0]
    jrow = _iota((nsb, TQ), 0)
    cur = lax.shift_right_logical(q0 + _iota((nsb, TQ), 1), 6)
    causal = jrow <= cur
    forced = (jrow == 0) | (jrow == cur) | (jrow == cur - 1)
    score = jnp.where(causal, jnp.where(forced, BIG, imp), NEG)
    rank = _selection_rank(score)
    unselected = jnp.where(causal, jnp.where(rank < float(N_SELECT), 0.0, NEG_FEATURE), NEG_FEATURE)
    mask_feat = jnp.concatenate([unselected, jnp.zeros((FEAT - nsb, TQ), jnp.float32)], axis=0).T
    q_sel = jnp.concatenate(
        [q, jnp.concatenate([mask_feat.astype(jnp.bfloat16)] * GROUP, axis=0)], axis=1)

    tk = TK_SLC
    diag_pair = (q0 + TQ - 1) // (2 * tk)

    def slc_scores(j):
        k0 = pl.multiple_of(j * tk, tk)
        k_aug = jnp.concatenate([ks_ref[0, 0, pl.ds(k0, tk), :], onehot_ref[pl.ds(k0, tk), :]], axis=1)
        return _dot_nt(k_aug, q_sel)

    def slc_update(j, slot, carry, near_diagonal):
        m, acc = carry
        k0 = pl.multiple_of(j * tk, tk)

        def rows(c):
            s = s_buf[slot, c:c + CHUNK, :]
            if near_diagonal:
                dist = (q0 + _iota((CHUNK, TQ), 1)) - (k0 + c + _iota((CHUNK, TQ), 0))
                s = s + _per_head(jnp.where(dist >= 0, 0.0, NEG))
            return s

        m_new = jnp.maximum(m, _chunked_max(rows, tk))
        alpha = jnp.exp2(m - m_new)
        p = _chunked_exp2(rows, tk, m_new)
        acc = alpha * acc + _dot(vst_ref[0, 0, :, pl.ds(k0, tk)], p)
        return m_new, acc

    s_buf[2] = slc_scores(2 * diag_pair)
    s_buf[3] = slc_scores(2 * diag_pair + 1)
    s_buf[0] = slc_scores(0)

    n = GROUP * TQ
    carry = (jnp.full((1, n), M_FLOOR, jnp.float32), jnp.zeros((V_ROWS, n), jnp.float32))
    carry = slc_update(2 * diag_pair, 2, carry, True)
    carry = slc_update(2 * diag_pair + 1, 3, carry, True)

    def slc_pair(jj, carry):
        j = 2 * jj
        s_buf[1] = slc_scores(j + 1)
        carry = slc_update(j, 0, carry, False)
        s_buf[0] = slc_scores(j + 2)
        return slc_update(j + 1, 1, carry, False)

    _, acc = lax.fori_loop(0, diag_pair, slc_pair, carry)
    o_slc = acc[:HEAD_DIM] / acc[HEAD_DIM:HEAD_DIM + 1]

    pieces = []
    for r in range(GROUP):
        sl = slice(r * TQ, (r + 1) * TQ)
        gate = [gt_ref[0, pl.ds(br * N_HEADS + g * GROUP + r, 1), :] for br in range(3)]
        pieces.append(gate[0] * o_cmp[:, sl] + gate[1] * o_slc[:, sl] + gate[2] * o_win[:, sl])
    _store_heads(jnp.concatenate(pieces, axis=1), sza_ref, oa_ref)


def _mixers(sinks, qa, qb, kc, vct, ks, vst, kw, vwt, kb, vbt, onehot, wimp, gt, sza, szb):
    B, _, _, S, _ = qa.shape
    n_cmp = kc.shape[1]
    q_spec = pl.BlockSpec((1, 1, GROUP, TQ, FEAT), lambda b, g, i: (b, g, 0, i, 0))
    seq_k = lambda n: pl.BlockSpec((1, 1, n, FEAT), lambda b, g, i: (b, g, 0, 0))
    seq_vt = lambda n: pl.BlockSpec((1, 1, V_ROWS, n), lambda b, g, i: (b, g, 0, 0))
    o_spec = pl.BlockSpec((1, TQ, GROUP * HEAD_DIM), lambda b, g, i: (b, i, g))
    o_shape = jax.ShapeDtypeStruct((B, S, WIDTH), jnp.bfloat16)
    return pl.pallas_call(
        _mixers_kernel,
        out_shape=(o_shape, o_shape),
        grid=(B, N_KV, S // TQ),
        in_specs=[pl.BlockSpec(memory_space=pltpu.SMEM), q_spec, q_spec,
                  pl.BlockSpec((1, n_cmp, FEAT), lambda b, g, i: (b * N_KV + g, 0, 0)),
                  pl.BlockSpec((1, HEAD_DIM, n_cmp), lambda b, g, i: (b * N_KV + g, 0, 0)),
                  seq_k(S), seq_vt(S), seq_k(kw.shape[2]), seq_vt(kw.shape[2]),
                  seq_k(kb.shape[2]), seq_vt(kb.shape[2]),
                  pl.BlockSpec((S, FEAT), lambda b, g, i: (0, 0)),
                  pl.BlockSpec(wimp.shape, lambda b, g, i: (0, 0)),
                  pl.BlockSpec((1, GATE_ROWS, TQ), lambda b, g, i: (b, 0, i)),
                  o_spec, o_spec],
        out_specs=(o_spec, o_spec),
        scratch_shapes=[pltpu.VMEM((4, TK_SLC, GROUP * TQ), jnp.float32)],
        compiler_params=pltpu.CompilerParams(
            dimension_semantics=("parallel", "parallel", "arbitrary"),
            vmem_limit_bytes=VMEM_LIMIT),
        name="mixers",
    )(sinks, qa, qb, kc, vct, ks, vst, kw, vwt, kb, vbt, onehot, wimp, gt, sza, szb)


def _out_proj_kernel(x_ref, oza_ref, ozb_ref, sma_ref, smb_ref, gate_ref, gpost_ref,
                     woa_ref, wob_ref, wout_ref, o_ref, woa_b, wob_b, wout_b):
    @pl.when((pl.program_id(0) == 0) & (pl.program_id(1) == 0))
    def _():
        woa_b[...] = woa_ref[...].astype(jnp.bfloat16)
        wob_b[...] = wob_ref[...].astype(jnp.bfloat16)
        wout_b[...] = wout_ref[...].astype(jnp.bfloat16)

    ya = _dot(oza_ref[0], woa_b[...])
    yb = _dot(ozb_ref[0], wob_b[...])
    y = sma_ref[0].astype(jnp.float32) * ya + smb_ref[0].astype(jnp.float32) * yb
    yo = _dot(y.astype(jnp.bfloat16), wout_b[...])
    ms = jnp.mean(yo * yo, axis=-1, keepdims=True)
    normed = yo * lax.rsqrt(ms + RMS_EPS) * gpost_ref[...]
    o_ref[0] = x_ref[0] + gate_ref[0] * normed


def _out_proj(x, oza, ozb, sma, smb, gate, gpost, woa, wob, wout):
    B, S, D = x.shape
    tm = TM_PROJ
    row_spec = lambda n: pl.BlockSpec((1, tm, n), lambda b, i: (b, i, 0))
    const = lambda shape: pl.BlockSpec(shape, lambda b, i: (0,) * len(shape))
    weight = lambda shape: pl.BlockSpec(shape, lambda b, i: (0, 0), pipeline_mode=pl.Buffered(1))
    return pl.pallas_call(
        _out_proj_kernel,
        out_shape=jax.ShapeDtypeStruct((B, S, D), jnp.float32),
        grid=(B, S // tm),
        in_specs=[row_spec(D), row_spec(WIDTH), row_spec(WIDTH), row_spec(D), row_spec(D),
                  pl.BlockSpec((1, 1, D), lambda b, i: (b, 0, 0)), const((1, D)),
                  weight((WIDTH, D)), weight((WIDTH, D)), weight((D, D))],
        out_specs=row_spec(D),
        scratch_shapes=[pltpu.VMEM((WIDTH, D), jnp.bfloat16), pltpu.VMEM((WIDTH, D), jnp.bfloat16),
                        pltpu.VMEM((D, D), jnp.bfloat16)],
        compiler_params=pltpu.CompilerParams(
            dimension_semantics=("arbitrary", "arbitrary"), vmem_limit_bytes=VMEM_LIMIT),
        name="out_proj",
    )(x, oza, ozb, sma, smb, gate, gpost, woa, wob, wout)


def _arrange_w_in(w):
    gate_end = WIDTH + 6 * KVW + N_GATES
    assert w.shape[1] + GATE_PAD - N_GATES == _C_END
    padded = jnp.concatenate(
        [w[:, :gate_end], jnp.zeros((w.shape[0], GATE_PAD - N_GATES), w.dtype), w[:, gate_end:]],
        axis=1)
    col = np.arange(_C_END)
    is_query = (col < _C_KVA) | ((col >= _C_QB) & (col < _C_KVB))
    col_scale = np.where(is_query, HEAD_DIM ** -0.5 * LOG2E, 1.0).astype(np.float32)
    return (padded * col_scale[None, :]).astype(jnp.bfloat16)


def _block_onehot(S):
    blk = jnp.arange(S, dtype=jnp.int32)[:, None] // SLC_BLOCK
    return (blk == jnp.arange(FEAT, dtype=jnp.int32)[None, :]).astype(jnp.bfloat16)


def kernel(x, c, w_ada, b_ada, g_pre, g_post, w_in, pe_cmp_k, pe_cmp_v, w_cmp_k1, w_cmp_k2,
           w_cmp_v1, w_cmp_v2, w_o_nsa, w_o_swa, w_out, sinks):
    B, S, D = x.shape
    depth = w_in.shape[0]
    bf = jnp.bfloat16
    for l in range(depth):
        c8 = jnp.pad(c, ((0, 8 - B), (0, 0)))
        mod = _adaln_mod(c8, w_ada[l], b_ada[l][None, :])[:B]
        shift, scale, gate = (mod[:, None, i * D:(i + 1) * D] for i in range(3))

        (qa, qb, t16, ks, vst, kw, vwt, kb, vbt, sza, szb, sma, smb, gt) = _in_proj(
            x, shift, scale, g_pre[l][None, :], _arrange_w_in(w_in[l]))

        rows = S // CMP_STRIDE
        t = t16.reshape(2, B * N_KV, rows, CMP_STRIDE * HEAD_DIM)
        pe = jnp.stack([pe_cmp_k[l], pe_cmp_v[l]]).reshape(2, 1, CMP_BLOCK * HEAD_DIM)
        w2 = jnp.pad(jnp.stack([w_cmp_k2[l], w_cmp_v2[l]]), ((0, 0), (0, 0), (0, FEAT - HEAD_DIM)))
        cmp_o, cmp_ot = _compress(t, pe, jnp.stack([w_cmp_k1[l], w_cmp_v1[l]]), w2)

        oza, ozb = _mixers(sinks[l], qa, qb, cmp_o[0], cmp_ot[1], ks, vst, kw, vwt, kb, vbt,
                           _block_onehot(S), _importance_matrix(S // SLC_BLOCK, rows), gt, sza, szb)

        x = _out_proj(x, oza, ozb, sma, smb, gate, g_post[l][None, :],
                      w_o_nsa[l], w_o_swa[l], w_out[l])
    return x
```

```python
import functools

import jax
import jax.numpy as jnp
import numpy as np
from jax import lax
from jax.experimental import pallas as pl
from jax.experimental.pallas import tpu as pltpu

D_MODEL = 1024
HEAD_DIM = 64
N_HEADS = 8
N_KV = 2
GROUP = N_HEADS // N_KV
CMP_BLOCK = 32
CMP_STRIDE = 16
CMP_HIDDEN = 256
SLC_BLOCK = 64
N_SELECT = 16
NSA_WINDOW = 512
SWA_WINDOW = 128
RMS_EPS = 1e-6
NEG = -1e30
BIG = 1e30
M_FLOOR = -1e20
NEG_FEATURE = -(2.0 ** 100)
PAD_KEY_BLOCK = -(2.0 ** 90)

WIDTH = N_HEADS * HEAD_DIM
KVW = N_KV * HEAD_DIM
FEAT = 128
V_ROWS = 80


def _bf16_pieces(x, n):
    out = []
    for _ in range(n):
        bits = np.array(x, np.float32).view(np.uint32)
        bits = (bits + np.uint32(0x7FFF) + ((bits >> np.uint32(16)) & np.uint32(1))) & np.uint32(0xFFFF0000)
        piece = float(bits.view(np.float32))
        out.append(piece)
        x -= piece
    return tuple(out)


LOG2E_PIECES = _bf16_pieces(1.4426950408889634, 3)
LOG2E = sum(LOG2E_PIECES)
N_PIECES = len(LOG2E_PIECES)
LANE_BLK = HEAD_DIM
LANE_POS = HEAD_DIM + N_PIECES
N_GATES = 3 * N_HEADS
GATE_PAD = 128
GATE_ROWS = 32

VMEM_LIMIT = 56 * 1024 * 1024

TM_PROJ = 512
TQ = 256
TK_SLC = 512
CHUNK = 16
W_CHUNK = 512
KEY_PAD = TM_PROJ
BAND_ALIGN = 128

_NT = (((1,), (1,)), ((), ()))


def _dot(a, b):
    return jnp.dot(a, b, preferred_element_type=jnp.float32)


def _dot_nt(a, b):
    return lax.dot_general(a, b, _NT, preferred_element_type=jnp.float32)


def _sigmoid(v):
    return 1.0 / (1.0 + jnp.exp(-v))


def _iota(shape, dim):
    return lax.broadcasted_iota(jnp.int32, shape, dim)


def _key_features(lane, pos):
    blk = lax.shift_right_logical(pos, 6).astype(jnp.float32)
    within = (pos & (SLC_BLOCK - 1)).astype(jnp.float32)
    return jnp.where(lane < LANE_POS, blk, jnp.where(lane < LANE_POS + N_PIECES, within, 0.0))


def _query_features(lane, slope):
    feat = jnp.zeros(lane.shape, jnp.float32)
    for i, piece in enumerate(LOG2E_PIECES):
        feat = jnp.where(lane == LANE_BLK + i, slope * SLC_BLOCK * piece, feat)
        feat = jnp.where(lane == LANE_POS + i, slope * piece, feat)
    return feat


def _adaln_kernel(c_ref, w_ref, b_ref, o_ref):
    o_ref[...] = jnp.dot(c_ref[...], w_ref[...], preferred_element_type=jnp.float32,
                         precision=lax.Precision.HIGHEST) + b_ref[...]


def _adaln_mod(c8, w, b):
    n = w.shape[1]
    bn = 1024
    return pl.pallas_call(
        _adaln_kernel,
        out_shape=jax.ShapeDtypeStruct((c8.shape[0], n), jnp.float32),
        grid=(n // bn,),
        in_specs=[pl.BlockSpec((c8.shape[0], D_MODEL), lambda j: (0, 0)),
                  pl.BlockSpec((D_MODEL, bn), lambda j: (0, j)),
                  pl.BlockSpec((1, bn), lambda j: (0, j))],
        out_specs=pl.BlockSpec((c8.shape[0], bn), lambda j: (0, j)),
        name="adaln_mod",
    )(c8, w, b)


_C_QA = 0
_C_KVA = _C_QA + WIDTH
_C_G = _C_KVA + 6 * KVW
_C_ZA = _C_G + GATE_PAD
_C_QB = _C_ZA + WIDTH
_C_KVB = _C_QB + WIDTH
_C_ZB = _C_KVB + 2 * KVW
_C_M = _C_ZB + WIDTH
_C_END = _C_M + 2 * D_MODEL


def _with_features(pair, odd, feat, lane):
    src = pltpu.roll(pair, HEAD_DIM, 1) if odd else pair
    return jnp.where(lane < HEAD_DIM, src, feat).astype(jnp.bfloat16)


def _weight_chunks(n_w):
    behind = GATE_PAD - N_GATES
    chunks = []
    for lo, hi in ((_C_QA, _C_G), (_C_ZA, _C_END)):
        c = lo
        while c < hi:
            seg_end = min(e for e in (_C_KVA, _C_G, _C_QB, _C_KVB, _C_END) if e > c)
            n = min(W_CHUNK, seg_end - c, hi - c)
            is_query = _C_QA <= c < _C_KVA or _C_QB <= c < _C_KVB
            chunks.append((c, n, c - (behind if lo == _C_ZA else 0), is_query))
            c += n
    assert chunks[-1][2] + chunks[-1][1] == n_w
    return chunks


def _load_projection_weight(wt_hbm, stage, sems, w_ref):
    q_scale = HEAD_DIM ** -0.5 * LOG2E
    n_gate_rows = -(-N_GATES // 8) * 8
    pieces = _weight_chunks(wt_hbm.shape[0])
    pieces.insert(3, (_C_G, n_gate_rows, _C_G, False))

    def copy(k):
        _, n, src, _ = pieces[k]
        return pltpu.make_async_copy(wt_hbm.at[pl.ds(src, n), :], stage.at[k % 2, pl.ds(0, n), :],
                                     sems.at[k % 2])

    copy(0).start()
    for k, (dst, n, _, is_query) in enumerate(pieces):
        if k + 1 < len(pieces):
            copy(k + 1).start()
        copy(k).wait()
        if dst == _C_G:
            block = stage[k % 2, :GATE_PAD, :]
            block = jnp.where(_iota(block.shape, 0) < N_GATES, block, 0.0)
            w_ref[:, _C_G:_C_ZA] = block.T.astype(jnp.bfloat16)
        else:
            block = stage[k % 2, :n, :]
            if is_query:
                block = block * q_scale
            w_ref[:, dst:dst + n] = block.T.astype(jnp.bfloat16)


def _in_proj_kernel(layer, x_ref, shift_ref, scale_ref, gpre_ref, wt_hbm,
                    _kw0, _vwt0, _kb0, _vbt0,
                    qa_ref, qb_ref, t16_ref, ks_ref, vst_ref, kw_ref, vwt_ref,
                    kb_ref, vbt_ref, sza_ref, szb_ref, sma_ref, smb_ref, gt_ref,
                    kv_scr, w_ref, w_stage, w_sems):
    @pl.when((pl.program_id(0) == 0) & (pl.program_id(1) == 0))
    def _():
        _load_projection_weight(wt_hbm.at[layer], w_stage, w_sems, w_ref)

    tm = x_ref.shape[1]
    xf = x_ref[0]
    ms = jnp.mean(xf * xf, axis=-1, keepdims=True)
    y = xf * lax.rsqrt(ms + RMS_EPS) * gpre_ref[...]
    h = y * (1.0 + scale_ref[0]) + shift_ref[0]
    hb = h.astype(jnp.bfloat16)

    lane = _iota((tm, FEAT), 1)
    pos = pl.program_id(1) * tm + _iota((tm, FEAT), 0)
    key_feat = _key_features(lane, pos)

    def store_queries(acc, q_ref):
        for head in range(N_HEADS):
            q_feat = _query_features(lane, 2.0 ** -(head + 1))
            g, r, odd = head // GROUP, head % GROUP, head % 2
            c = (head // 2) * FEAT
            q_ref[0, g, r] = _with_features(acc[:, c:c + FEAT], odd, q_feat, lane)

    def store_values(acc, c0, vt_ref):
        vt = acc[:, c0:c0 + KVW].T
        ones_rows = jnp.where(_iota((V_ROWS - HEAD_DIM, tm), 0) == 0, 1.0, 0.0).astype(jnp.bfloat16)
        for g in range(N_KV):
            rows = vt[g * HEAD_DIM:(g + 1) * HEAD_DIM].astype(jnp.bfloat16)
            vt_ref[0, g] = jnp.concatenate([rows, ones_rows], axis=0)

    acc = _dot(hb, w_ref[:, _C_QA:_C_G])
    store_queries(acc, qa_ref)
    acc = acc[:, _C_KVA:]
    for a in range(2):
        kv_scr[a] = acc[:, a * KVW:(a + 1) * KVW]
        kv_scr[2 + a] = pltpu.roll(acc[:, a * KVW:(a + 1) * KVW], HEAD_DIM, 1)
    for l in range(CMP_STRIDE):
        for a in range(2):
            plain = kv_scr[a, pl.ds(l, tm // CMP_STRIDE, stride=CMP_STRIDE), :]
            rolled = kv_scr[2 + a, pl.ds(l, tm // CMP_STRIDE, stride=CMP_STRIDE), :]
            half = (l % 2) * HEAD_DIM
            for g in range(N_KV):
                tok = (plain if g % 2 == l % 2 else rolled)[:, half:half + HEAD_DIM]
                t16_ref[a, 0, g, :, l * HEAD_DIM:(l + 1) * HEAD_DIM] = tok.astype(jnp.bfloat16)
    for g in range(N_KV):
        ks_ref[0, g] = _with_features(acc[:, 2 * KVW:3 * KVW], g, key_feat, lane)
        kw_ref[0, g] = _with_features(acc[:, 4 * KVW:5 * KVW], g, key_feat, lane)
    store_values(acc, 3 * KVW, vst_ref)
    store_values(acc, 5 * KVW, vwt_ref)

    acc = _dot(hb, w_ref[:, _C_G:_C_ZA])
    gt_ref[0] = _sigmoid(acc).T[:GATE_ROWS]

    acc = _dot(hb, w_ref[:, _C_ZA:_C_M])
    za = acc[:, :WIDTH]
    sza_ref[0] = (za * _sigmoid(za)).astype(jnp.bfloat16)
    store_queries(acc[:, _C_QB - _C_ZA:_C_KVB - _C_ZA], qb_ref)
    kvb = acc[:, _C_KVB - _C_ZA:_C_ZB - _C_ZA]
    for g in range(N_KV):
        kb_ref[0, g] = _with_features(kvb[:, :KVW], g, key_feat, lane)
    store_values(kvb, KVW, vbt_ref)
    zb = acc[:, _C_ZB - _C_ZA:]
    szb_ref[0] = (zb * _sigmoid(zb)).astype(jnp.bfloat16)

    acc = _dot(hb, w_ref[:, _C_M:_C_M + D_MODEL])
    sma_ref[0] = _sigmoid(acc).astype(jnp.bfloat16)
    acc = _dot(hb, w_ref[:, _C_M + D_MODEL:_C_END])
    smb_ref[0] = _sigmoid(acc).astype(jnp.bfloat16)


def _padded_init(B, S):
    pad_row = jnp.zeros((FEAT,), jnp.float32).at[LANE_BLK:LANE_POS].set(PAD_KEY_BLOCK)
    k_init = jnp.broadcast_to(pad_row.astype(jnp.bfloat16), (B, N_KV, KEY_PAD + S, FEAT))
    vt_init = jnp.zeros((B, N_KV, V_ROWS, KEY_PAD + S), jnp.bfloat16)
    return k_init, vt_init


def _in_proj(x, shift, scale, gpre, w_in_t, layer):
    B, S, D = x.shape
    assert w_in_t.shape[1] + GATE_PAD - N_GATES == _C_END
    tm = TM_PROJ
    assert KEY_PAD == tm and KEY_PAD >= max(NSA_WINDOW, SWA_WINDOW)
    bf = jnp.bfloat16
    q_shape = jax.ShapeDtypeStruct((B, N_KV, GROUP, S, FEAT), bf)
    t16_shape = jax.ShapeDtypeStruct((2, B, N_KV, S // CMP_STRIDE, CMP_STRIDE * HEAD_DIM), bf)
    k_shape = jax.ShapeDtypeStruct((B, N_KV, S, FEAT), bf)
    vt_shape = jax.ShapeDtypeStruct((B, N_KV, V_ROWS, S), bf)
    q_spec = pl.BlockSpec((1, N_KV, GROUP, tm, FEAT), lambda b, i: (b, 0, 0, i, 0))
    t16_spec = pl.BlockSpec((2, 1, N_KV, tm // CMP_STRIDE, CMP_STRIDE * HEAD_DIM),
                            lambda b, i: (0, b, 0, i, 0))
    k_spec = pl.BlockSpec((1, N_KV, tm, FEAT), lambda b, i: (b, 0, i, 0))
    vt_spec = pl.BlockSpec((1, N_KV, V_ROWS, tm), lambda b, i: (b, 0, 0, i))
    kp_shape = jax.ShapeDtypeStruct((B, N_KV, KEY_PAD + S, FEAT), bf)
    vtp_shape = jax.ShapeDtypeStruct((B, N_KV, V_ROWS, KEY_PAD + S), bf)
    kp_spec = pl.BlockSpec((1, N_KV, tm, FEAT), lambda b, i: (b, 0, i + 1, 0))
    vtp_spec = pl.BlockSpec((1, N_KV, V_ROWS, tm), lambda b, i: (b, 0, 0, i + 1))
    k_init, vt_init = _padded_init(B, S)
    any_spec = pl.BlockSpec(memory_space=pl.ANY)
    row_spec = lambda n: pl.BlockSpec((1, tm, n), lambda b, i: (b, i, 0))
    vec_spec = pl.BlockSpec((1, 1, D), lambda b, i: (b, 0, 0))
    return pl.pallas_call(
        functools.partial(_in_proj_kernel, layer),
        out_shape=(q_shape, q_shape, t16_shape, k_shape, vt_shape, kp_shape, vtp_shape,
                   kp_shape, vtp_shape,
                   jax.ShapeDtypeStruct((B, S, WIDTH), bf), jax.ShapeDtypeStruct((B, S, WIDTH), bf),
                   jax.ShapeDtypeStruct((B, S, D), bf), jax.ShapeDtypeStruct((B, S, D), bf),
                   jax.ShapeDtypeStruct((B, GATE_ROWS, S), jnp.float32)),
        grid=(B, S // tm),
        in_specs=[row_spec(D), vec_spec, vec_spec,
                  pl.BlockSpec((1, D), lambda b, i: (0, 0)),
                  any_spec, any_spec, any_spec, any_spec, any_spec],
        input_output_aliases={5: 5, 6: 6, 7: 7, 8: 8},
        out_specs=(q_spec, q_spec, t16_spec, k_spec, vt_spec, kp_spec, vtp_spec,
                   kp_spec, vtp_spec,
                   row_spec(WIDTH), row_spec(WIDTH), row_spec(D), row_spec(D),
                   pl.BlockSpec((1, GATE_ROWS, tm), lambda b, i: (b, 0, i))),
        scratch_shapes=[pltpu.VMEM((4, tm, KVW), jnp.float32),
                        pltpu.VMEM((D, _C_END), bf),
                        pltpu.VMEM((2, W_CHUNK, D), jnp.float32),
                        pltpu.SemaphoreType.DMA((2,))],
        compiler_params=pltpu.CompilerParams(
            dimension_semantics=("arbitrary", "arbitrary"), vmem_limit_bytes=VMEM_LIMIT),
        name="in_proj",
    )(x, shift, scale, gpre, w_in_t, k_init, vt_init, k_init, vt_init)


def _compress_kernel(t_ref, pe_ref, w1_ref, w2_ref, o_ref, ot_ref, w1b_scr, bias_scr):
    half = CMP_STRIDE * HEAD_DIM

    @pl.when(pl.program_id(1) == 0)
    def _():
        w1 = w1_ref[0]
        w1b_scr[...] = w1.astype(jnp.bfloat16)
        pe = jnp.broadcast_to(pe_ref[0], (bias_scr.shape[0], 2 * half))
        bias_scr[...] = jnp.dot(pe, w1, preferred_element_type=jnp.float32,
                                precision=lax.Precision.HIGHEST)

    t = t_ref[0, 0]
    first = _dot(t, w1b_scr[:half])
    second = _dot(t, w1b_scr[half:])
    n = second.shape[0]
    second = pltpu.roll(second, n - 1, 0)
    pre = first + second + bias_scr[0:1]
    hid = pre * _sigmoid(pre)
    out = _dot(hid.astype(jnp.bfloat16), w2_ref[0].astype(jnp.bfloat16))
    lane = _iota((n, FEAT), 1)
    c = _iota((n, FEAT), 0)
    end = c * CMP_STRIDE + (CMP_BLOCK - 1)
    o_ref[0, 0] = jnp.where(lane < HEAD_DIM, out, _key_features(lane, end)).astype(jnp.bfloat16)
    ot_ref[0, 0] = out.T[:HEAD_DIM].astype(jnp.bfloat16)


def _compress(t, pe, w1, w2):
    _, BG, n, wdt = t.shape
    return pl.pallas_call(
        _compress_kernel,
        out_shape=(jax.ShapeDtypeStruct((2, BG, n, FEAT), jnp.bfloat16),
                   jax.ShapeDtypeStruct((2, BG, HEAD_DIM, n), jnp.bfloat16)),
        grid=(2, BG),
        in_specs=[pl.BlockSpec((1, 1, n, wdt), lambda a, i: (a, i, 0, 0)),
                  pl.BlockSpec((1, 1, 2 * wdt), lambda a, i: (a, 0, 0)),
                  pl.BlockSpec((1, 2 * wdt, CMP_HIDDEN), lambda a, i: (a, 0, 0)),
                  pl.BlockSpec((1, CMP_HIDDEN, FEAT), lambda a, i: (a, 0, 0))],
        out_specs=(pl.BlockSpec((1, 1, n, FEAT), lambda a, i: (a, i, 0, 0)),
                   pl.BlockSpec((1, 1, HEAD_DIM, n), lambda a, i: (a, i, 0, 0))),
        scratch_shapes=[pltpu.VMEM((2 * wdt, CMP_HIDDEN), jnp.bfloat16),
                        pltpu.VMEM((8, CMP_HIDDEN), jnp.float32)],
        compiler_params=pltpu.CompilerParams(dimension_semantics=("arbitrary", "arbitrary")),
        name="compress",
    )(t, pe, w1, w2)


def _head_slope(g, r):
    return jnp.where(g == 0, 1.0, 2.0 ** -GROUP).astype(jnp.float32) * (2.0 ** -(r + 1))


def _chunked_max(rows, n_rows):
    m8 = None
    for c in range(0, n_rows, CHUNK):
        chunk = rows(c)
        for r in range(0, CHUNK, 8):
            m8 = chunk[r:r + 8] if m8 is None else jnp.maximum(m8, chunk[r:r + 8])
    return jnp.max(m8, axis=0, keepdims=True)


def _chunked_exp2(rows, n_rows, m):
    return jnp.concatenate(
        [jnp.exp2(rows(c) - m).astype(jnp.bfloat16) for c in range(0, n_rows, CHUNK)], axis=0)


def _per_head(x):
    return jnp.concatenate([x] * GROUP, axis=1)


def _band_scores(q, k_ref, q0, window):
    start = pl.multiple_of(q0 + (KEY_PAD - window), BAND_ALIGN)
    return _dot_nt(k_ref[0, 0, pl.ds(start, window + TQ), :], q)


def _band_finish(st, vt_ref, q0, window, sink_row):
    n_keys = window + TQ
    start = pl.multiple_of(q0 + (KEY_PAD - window), BAND_ALIGN)
    col = _iota((CHUNK, TQ), 1)

    def rows(c):
        part = st[c:c + CHUNK]
        row = c + _iota((CHUNK, TQ), 0)
        if c < TQ:
            part = part + _per_head(jnp.where(row > col, 0.0, NEG))
        if c + CHUNK - 1 > window:
            part = part + _per_head(jnp.where(row <= col + window, 0.0, NEG))
        return part

    m = _chunked_max(rows, n_keys)
    if sink_row is not None:
        m = jnp.maximum(m, sink_row)
    p = _chunked_exp2(rows, n_keys, m)
    o = _dot(vt_ref[0, 0, :, pl.ds(start, n_keys)], p)
    l = o[HEAD_DIM:HEAD_DIM + 1]
    if sink_row is not None:
        l = l + jnp.exp2(sink_row - m)
    return o[:HEAD_DIM] / l


def _store_heads(o_t, sz_ref, o_ref):
    stacked = jnp.concatenate([o_t[:, r * TQ:(r + 1) * TQ] for r in range(GROUP)], axis=0)
    o_ref[0] = (stacked.T * sz_ref[0].astype(jnp.float32)).astype(jnp.bfloat16)


def _importance_matrix(nsb, n_cmp):
    ratio = SLC_BLOCK // CMP_STRIDE
    span = CMP_BLOCK // CMP_STRIDE
    off = ratio * np.arange(nsb)[:, None] - np.arange(n_cmp)[None, :]
    w = sum((off == mm + nn).astype(np.float32) for mm in range(ratio) for nn in range(span))
    w[:, n_cmp - 1] = 0.0
    return jnp.asarray(w, jnp.bfloat16)


def _selection_rank(score):
    nsb = score.shape[0]
    groups = [score[8 * a:8 * a + 8] for a in range(nsb // 8)]
    sub = _iota((8, TQ), 0)
    rank = [jnp.zeros((8, TQ), jnp.float32) for _ in groups]
    for i in range(nsb):
        row = jnp.broadcast_to(score[i:i + 1, :], (8, TQ))
        for a, grp in enumerate(groups):
            if a < i // 8:
                ahead = jnp.where(row > grp, 1.0, 0.0)
            elif a > i // 8:
                ahead = jnp.where(row >= grp, 1.0, 0.0)
            else:
                ahead = jnp.where(sub > i % 8, jnp.where(row >= grp, 1.0, 0.0),
                                  jnp.where(row > grp, 1.0, 0.0))
            rank[a] = rank[a] + ahead
    return jnp.concatenate(rank, axis=0)


def _mixers_kernel(sinks_ref, qa_ref, qb_ref, kc_ref, vct_ref, ks_ref, vst_ref, kw_ref, vwt_ref,
                   kb_ref, vbt_ref, onehot_ref, wimp_ref, gt_ref, sza_ref, szb_ref, oa_ref, ob_ref,
                   s_buf):
    g = pl.program_id(1)
    qi = pl.program_id(2)
    q0 = qi * TQ
    q = qa_ref[0, 0].reshape(GROUP * TQ, FEAT)

    n_cmp = kc_ref.shape[1]
    st_cmp = _dot_nt(kc_ref[0], q)
    st_swa = _band_scores(qb_ref[0, 0].reshape(GROUP * TQ, FEAT), kb_ref, q0, SWA_WINDOW)
    st_win = _band_scores(q, kw_ref, q0, NSA_WINDOW)

    qpos_c = q0 + _iota((n_cmp, TQ), 1)
    end_c = _iota((n_cmp, TQ), 0) * CMP_STRIDE + (CMP_BLOCK - 1)
    hidden = jnp.where(qpos_c >= end_c, 0.0, NEG)
    any_valid = (qpos_c[0:1] >= CMP_BLOCK - 1).astype(jnp.float32)
    p_sum = jnp.zeros((n_cmp, TQ), jnp.float32)
    o_cmp = []
    for r in range(GROUP):
        sr = st_cmp[:, r * TQ:(r + 1) * TQ] + hidden
        er = jnp.exp2(sr - jnp.max(sr, axis=0, keepdims=True))
        inv = any_valid / jnp.sum(er, axis=0, keepdims=True)
        p_sum = p_sum + er * inv
        o_cmp.append(_dot(vct_ref[0], er.astype(jnp.bfloat16)) * inv)
    o_cmp = jnp.concatenate(o_cmp, axis=1)

    wimp = wimp_ref[...]
    p_hi = p_sum.astype(jnp.bfloat16)
    rem = p_sum - p_hi.astype(jnp.float32)
    p_mid = rem.astype(jnp.bfloat16)
    p_lo = (rem - p_mid.astype(jnp.float32)).astype(jnp.bfloat16)
    imp = _dot(wimp, p_hi) + _dot(wimp, p_mid) + _dot(wimp, p_lo)

    qpos = (q0 + _iota((1, TQ), 1)).astype(jnp.float32)
    qpos_log2 = sum(qpos * piece for piece in LOG2E_PIECES)
    sink_row = jnp.concatenate(
        [sinks_ref[g * GROUP + r] * LOG2E + _head_slope(g, r) * qpos_log2 for r in range(GROUP)],
        axis=1)
    _store_heads(_band_finish(st_swa, vbt_ref, q0, SWA_WINDOW, sink_row), szb_ref, ob_ref)

    o_win = _band_finish(st_win, vwt_ref, q0, NSA_WINDOW, None)

    nsb = imp.shape[0]
    jrow = _iota((nsb, TQ), 0)
    cur = lax.shift_right_logical(q0 + _iota((nsb, TQ), 1), 6)
    causal = jrow <= cur
    forced = (jrow == 0) | (jrow == cur) | (jrow == cur - 1)
    score = jnp.where(causal, jnp.where(forced, BIG, imp), NEG)
    rank = _selection_rank(score)
    unselected = jnp.where(causal, jnp.where(rank < float(N_SELECT), 0.0, NEG_FEATURE), NEG_FEATURE)
    mask_feat = jnp.concatenate([unselected, jnp.zeros((FEAT - nsb, TQ), jnp.float32)], axis=0).T
    q_sel = jnp.concatenate(
        [q, jnp.concatenate([mask_feat.astype(jnp.bfloat16)] * GROUP, axis=0)], axis=1)

    tk = TK_SLC
    diag_pair = (q0 + TQ - 1) // (2 * tk)

    def slc_scores(j):
        k0 = pl.multiple_of(j * tk, tk)
        k_aug = jnp.concatenate([ks_ref[0, 0, pl.ds(k0, tk), :], onehot_ref[pl.ds(k0, tk), :]], axis=1)
        return _dot_nt(k_aug, q_sel)

    def slc_update(j, slot, carry, near_diagonal):
        m, acc = carry
        k0 = pl.multiple_of(j * tk, tk)

        def rows(c):
            s = s_buf[slot, c:c + CHUNK, :]
            if near_diagonal:
                dist = (q0 + _iota((CHUNK, TQ), 1)) - (k0 + c + _iota((CHUNK, TQ), 0))
                s = s + _per_head(jnp.where(dist >= 0, 0.0, NEG))
            return s

        m_new = jnp.maximum(m, _chunked_max(rows, tk))
        alpha = jnp.exp2(m - m_new)
        p = _chunked_exp2(rows, tk, m_new)
        acc = alpha * acc + _dot(vst_ref[0, 0, :, pl.ds(k0, tk)], p)
        return m_new, acc

    s_buf[2] = slc_scores(2 * diag_pair)
    s_buf[3] = slc_scores(2 * diag_pair + 1)
    s_buf[0] = slc_scores(0)

    n = GROUP * TQ
    carry = (jnp.full((1, n), M_FLOOR, jnp.float32), jnp.zeros((V_ROWS, n), jnp.float32))
    carry = slc_update(2 * diag_pair, 2, carry, True)
    carry = slc_update(2 * diag_pair + 1, 3, carry, True)

    def slc_pair(jj, carry):
        j = 2 * jj
        s_buf[1] = slc_scores(j + 1)
        carry = slc_update(j, 0, carry, False)
        s_buf[0] = slc_scores(j + 2)
        return slc_update(j + 1, 1, carry, False)

    _, acc = lax.fori_loop(0, diag_pair, slc_pair, carry)
    o_slc = acc[:HEAD_DIM] / acc[HEAD_DIM:HEAD_DIM + 1]

    pieces = []
    for r in range(GROUP):
        sl = slice(r * TQ, (r + 1) * TQ)
        gate = [gt_ref[0, pl.ds(br * N_HEADS + g * GROUP + r, 1), :] for br in range(3)]
        pieces.append(gate[0] * o_cmp[:, sl] + gate[1] * o_slc[:, sl] + gate[2] * o_win[:, sl])
    _store_heads(jnp.concatenate(pieces, axis=1), sza_ref, oa_ref)


def _mixers(sinks, qa, qb, kc, vct, ks, vst, kw, vwt, kb, vbt, onehot, wimp, gt, sza, szb):
    B, _, _, S, _ = qa.shape
    n_cmp = kc.shape[1]
    q_spec = pl.BlockSpec((1, 1, GROUP, TQ, FEAT), lambda b, g, i: (b, g, 0, i, 0))
    seq_k = lambda n: pl.BlockSpec((1, 1, n, FEAT), lambda b, g, i: (b, g, 0, 0))
    seq_vt = lambda n: pl.BlockSpec((1, 1, V_ROWS, n), lambda b, g, i: (b, g, 0, 0))
    o_spec = pl.BlockSpec((1, TQ, GROUP * HEAD_DIM), lambda b, g, i: (b, i, g))
    o_shape = jax.ShapeDtypeStruct((B, S, WIDTH), jnp.bfloat16)
    return pl.pallas_call(
        _mixers_kernel,
        out_shape=(o_shape, o_shape),
        grid=(B, N_KV, S // TQ),
        in_specs=[pl.BlockSpec(memory_space=pltpu.SMEM), q_spec, q_spec,
                  pl.BlockSpec((1, n_cmp, FEAT), lambda b, g, i: (b * N_KV + g, 0, 0)),
                  pl.BlockSpec((1, HEAD_DIM, n_cmp), lambda b, g, i: (b * N_KV + g, 0, 0)),
                  seq_k(S), seq_vt(S), seq_k(kw.shape[2]), seq_vt(kw.shape[2]),
                  seq_k(kb.shape[2]), seq_vt(kb.shape[2]),
                  pl.BlockSpec((S, FEAT), lambda b, g, i: (0, 0)),
                  pl.BlockSpec(wimp.shape, lambda b, g, i: (0, 0)),
                  pl.BlockSpec((1, GATE_ROWS, TQ), lambda b, g, i: (b, 0, i)),
                  o_spec, o_spec],
        out_specs=(o_spec, o_spec),
        scratch_shapes=[pltpu.VMEM((4, TK_SLC, GROUP * TQ), jnp.float32)],
        compiler_params=pltpu.CompilerParams(
            dimension_semantics=("parallel", "parallel", "arbitrary"),
            vmem_limit_bytes=VMEM_LIMIT),
        name="mixers",
    )(sinks, qa, qb, kc, vct, ks, vst, kw, vwt, kb, vbt, onehot, wimp, gt, sza, szb)


def _out_proj_kernel(x_ref, oza_ref, ozb_ref, sma_ref, smb_ref, gate_ref, gpost_ref,
                     woa_ref, wob_ref, wout_ref, o_ref, woa_b, wob_b, wout_b):
    @pl.when((pl.program_id(0) == 0) & (pl.program_id(1) == 0))
    def _():
        woa_b[...] = woa_ref[...].astype(jnp.bfloat16)
        wob_b[...] = wob_ref[...].astype(jnp.bfloat16)
        wout_b[...] = wout_ref[...].astype(jnp.bfloat16)

    ya = _dot(oza_ref[0], woa_b[...])
    yb = _dot(ozb_ref[0], wob_b[...])
    y = sma_ref[0].astype(jnp.float32) * ya + smb_ref[0].astype(jnp.float32) * yb
    yo = _dot(y.astype(jnp.bfloat16), wout_b[...])
    ms = jnp.mean(yo * yo, axis=-1, keepdims=True)
    normed = yo * lax.rsqrt(ms + RMS_EPS) * gpost_ref[...]
    o_ref[0] = x_ref[0] + gate_ref[0] * normed


def _out_proj(x, oza, ozb, sma, smb, gate, gpost, woa, wob, wout):
    B, S, D = x.shape
    tm = TM_PROJ
    row_spec = lambda n: pl.BlockSpec((1, tm, n), lambda b, i: (b, i, 0))
    const = lambda shape: pl.BlockSpec(shape, lambda b, i: (0,) * len(shape))
    weight = lambda shape: pl.BlockSpec(shape, lambda b, i: (0, 0), pipeline_mode=pl.Buffered(1))
    return pl.pallas_call(
        _out_proj_kernel,
        out_shape=jax.ShapeDtypeStruct((B, S, D), jnp.float32),
        grid=(B, S // tm),
        in_specs=[row_spec(D), row_spec(WIDTH), row_spec(WIDTH), row_spec(D), row_spec(D),
                  pl.BlockSpec((1, 1, D), lambda b, i: (b, 0, 0)), const((1, D)),
                  weight((WIDTH, D)), weight((WIDTH, D)), weight((D, D))],
        out_specs=row_spec(D),
        scratch_shapes=[pltpu.VMEM((WIDTH, D), jnp.bfloat16), pltpu.VMEM((WIDTH, D), jnp.bfloat16),
                        pltpu.VMEM((D, D), jnp.bfloat16)],
        compiler_params=pltpu.CompilerParams(
            dimension_semantics=("arbitrary", "arbitrary"), vmem_limit_bytes=VMEM_LIMIT),
        name="out_proj",
    )(x, oza, ozb, sma, smb, gate, gpost, woa, wob, wout)


def _block_onehot(S):
    blk = jnp.arange(S, dtype=jnp.int32)[:, None] // SLC_BLOCK
    return (blk == jnp.arange(FEAT, dtype=jnp.int32)[None, :]).astype(jnp.bfloat16)


def kernel(x, c, w_ada, b_ada, g_pre, g_post, w_in, pe_cmp_k, pe_cmp_v, w_cmp_k1, w_cmp_k2,
           w_cmp_v1, w_cmp_v2, w_o_nsa, w_o_swa, w_out, sinks):
    B, S, D = x.shape
    depth = w_in.shape[0]
    for l in range(depth):
        c8 = jnp.pad(c, ((0, 8 - B), (0, 0)))
        mod = _adaln_mod(c8, w_ada[l], b_ada[l][None, :])[:B]
        shift, scale, gate = (mod[:, None, i * D:(i + 1) * D] for i in range(3))

        (qa, qb, t16, ks, vst, kw, vwt, kb, vbt, sza, szb, sma, smb, gt) = _in_proj(
            x, shift, scale, g_pre[l][None, :], jnp.swapaxes(w_in, 1, 2), l)

        rows = S // CMP_STRIDE
        t = t16.reshape(2, B * N_KV, rows, CMP_STRIDE * HEAD_DIM)
        pe = jnp.stack([pe_cmp_k[l], pe_cmp_v[l]]).reshape(2, 1, CMP_BLOCK * HEAD_DIM)
        w2 = jnp.pad(jnp.stack([w_cmp_k2[l], w_cmp_v2[l]]), ((0, 0), (0, 0), (0, FEAT - HEAD_DIM)))
        cmp_o, cmp_ot = _compress(t, pe, jnp.stack([w_cmp_k1[l], w_cmp_v1[l]]), w2)

        oza, ozb = _mixers(sinks[l], qa, qb, cmp_o[0], cmp_ot[1], ks, vst, kw, vwt, kb, vbt,
                           _block_onehot(S), _importance_matrix(S // SLC_BLOCK, rows), gt, sza, szb)

        x = _out_proj(x, oza, ozb, sma, smb, gate, g_post[l][None, :],
                      w_o_nsa[l], w_o_swa[l], w_out[l])
    return x
```

```python
import functools

import jax
import jax.numpy as jnp
import numpy as np
from jax import lax
from jax.experimental import pallas as pl
from jax.experimental.pallas import tpu as pltpu

D_MODEL = 1024
HEAD_DIM = 64
N_HEADS = 8
N_KV = 2
GROUP = N_HEADS // N_KV
CMP_BLOCK = 32
CMP_STRIDE = 16
CMP_HIDDEN = 256
SLC_BLOCK = 64
N_SELECT = 16
NSA_WINDOW = 512
SWA_WINDOW = 128
RMS_EPS = 1e-6
NEG = -1e30
BIG = 1e30
M_FLOOR = -1e20
NEG_FEATURE = -(2.0 ** 100)
PAD_KEY_BLOCK = -(2.0 ** 90)

WIDTH = N_HEADS * HEAD_DIM
KVW = N_KV * HEAD_DIM
FEAT = 128
V_ROWS = 80


def _bf16_pieces(x, n):
    out = []
    for _ in range(n):
        bits = np.array(x, np.float32).view(np.uint32)
        bits = (bits + np.uint32(0x7FFF) + ((bits >> np.uint32(16)) & np.uint32(1))) & np.uint32(0xFFFF0000)
        piece = float(bits.view(np.float32))
        out.append(piece)
        x -= piece
    return tuple(out)


LOG2E_PIECES = _bf16_pieces(1.4426950408889634, 3)
LOG2E = sum(LOG2E_PIECES)
N_PIECES = len(LOG2E_PIECES)
LANE_BLK = HEAD_DIM
LANE_POS = HEAD_DIM + N_PIECES
N_GATES = 3 * N_HEADS
GATE_PAD = 128
GATE_ROWS = 32

VMEM_LIMIT = 56 * 1024 * 1024

TM_PROJ = 512
TQ = 256
TK_SLC = 512
CHUNK = 16
W_CHUNK = 512
KEY_PAD = TM_PROJ
BAND_ALIGN = 128

_NT = (((1,), (1,)), ((), ()))


def _dot(a, b):
    return jnp.dot(a, b, preferred_element_type=jnp.float32)


def _dot_nt(a, b):
    return lax.dot_general(a, b, _NT, preferred_element_type=jnp.float32)


def _sigmoid(v):
    return 1.0 / (1.0 + jnp.exp(-v))


def _iota(shape, dim):
    return lax.broadcasted_iota(jnp.int32, shape, dim)


def _key_features(lane, pos):
    blk = lax.shift_right_logical(pos, 6).astype(jnp.float32)
    within = (pos & (SLC_BLOCK - 1)).astype(jnp.float32)
    return jnp.where(lane < LANE_POS, blk, jnp.where(lane < LANE_POS + N_PIECES, within, 0.0))


def _query_features(lane, slope):
    feat = jnp.zeros(lane.shape, jnp.float32)
    for i, piece in enumerate(LOG2E_PIECES):
        feat = jnp.where(lane == LANE_BLK + i, slope * SLC_BLOCK * piece, feat)
        feat = jnp.where(lane == LANE_POS + i, slope * piece, feat)
    return feat


def _adaln_kernel(c_ref, w_ref, b_ref, o_ref):
    o_ref[...] = jnp.dot(c_ref[...], w_ref[...], preferred_element_type=jnp.float32,
                         precision=lax.Precision.HIGHEST) + b_ref[...]


def _adaln_mod(c8, w, b):
    n = w.shape[1]
    bn = 1024
    return pl.pallas_call(
        _adaln_kernel,
        out_shape=jax.ShapeDtypeStruct((c8.shape[0], n), jnp.float32),
        grid=(n // bn,),
        in_specs=[pl.BlockSpec((c8.shape[0], D_MODEL), lambda j: (0, 0)),
                  pl.BlockSpec((D_MODEL, bn), lambda j: (0, j)),
                  pl.BlockSpec((1, bn), lambda j: (0, j))],
        out_specs=pl.BlockSpec((c8.shape[0], bn), lambda j: (0, j)),
        name="adaln_mod",
    )(c8, w, b)


_C_QA = 0
_C_KVA = _C_QA + WIDTH
_C_G = _C_KVA + 6 * KVW
_C_ZA = _C_G + GATE_PAD
_C_QB = _C_ZA + WIDTH
_C_KVB = _C_QB + WIDTH
_C_ZB = _C_KVB + 2 * KVW
_C_M = _C_ZB + WIDTH
_C_END = _C_M + 2 * D_MODEL


def _with_features(pair, odd, feat, lane):
    src = pltpu.roll(pair, HEAD_DIM, 1) if odd else pair
    return jnp.where(lane < HEAD_DIM, src, feat).astype(jnp.bfloat16)


def _weight_chunks(n_w):
    behind = GATE_PAD - N_GATES
    chunks = []
    for lo, hi in ((_C_QA, _C_G), (_C_ZA, _C_END)):
        c = lo
        while c < hi:
            seg_end = min(e for e in (_C_KVA, _C_G, _C_QB, _C_KVB, _C_END) if e > c)
            n = min(W_CHUNK, seg_end - c, hi - c)
            is_query = _C_QA <= c < _C_KVA or _C_QB <= c < _C_KVB
            chunks.append((c, n, c - (behind if lo == _C_ZA else 0), is_query))
            c += n
    assert chunks[-1][2] + chunks[-1][1] == n_w
    return chunks


def _load_projection_weight(wt_hbm, stage, sems, w_ref):
    q_scale = HEAD_DIM ** -0.5 * LOG2E
    n_gate_rows = -(-N_GATES // 8) * 8
    pieces = _weight_chunks(wt_hbm.shape[0])
    pieces.insert(3, (_C_G, n_gate_rows, _C_G, False))

    def copy(k):
        _, n, src, _ = pieces[k]
        return pltpu.make_async_copy(wt_hbm.at[pl.ds(src, n), :], stage.at[k % 2, pl.ds(0, n), :],
                                     sems.at[k % 2])

    copy(0).start()
    for k, (dst, n, _, is_query) in enumerate(pieces):
        if k + 1 < len(pieces):
            copy(k + 1).start()
        copy(k).wait()
        if dst == _C_G:
            block = stage[k % 2, :GATE_PAD, :]
            block = jnp.where(_iota(block.shape, 0) < N_GATES, block, 0.0)
            w_ref[:, _C_G:_C_ZA] = block.T.astype(jnp.bfloat16)
        else:
            block = stage[k % 2, :n, :]
            if is_query:
                block = block * q_scale
            w_ref[:, dst:dst + n] = block.T.astype(jnp.bfloat16)


def _in_proj_kernel(layer, x_ref, shift_ref, scale_ref, gpre_ref, wt_hbm,
                    _kw0, _vwt0, _kb0, _vbt0,
                    qa_ref, qb_ref, t16_ref, ks_ref, vst_ref, kw_ref, vwt_ref,
                    kb_ref, vbt_ref, sza_ref, szb_ref, sma_ref, smb_ref, gt_ref,
                    kv_scr, w_ref, w_stage, w_sems):
    @pl.when((pl.program_id(0) == 0) & (pl.program_id(1) == 0))
    def _():
        _load_projection_weight(wt_hbm.at[layer], w_stage, w_sems, w_ref)

    hm = x_ref.shape[1] // 2
    for half in range(2):
        _project_rows(half * hm, hm, x_ref, shift_ref, scale_ref, gpre_ref, w_ref,
                      qa_ref, qb_ref, t16_ref, ks_ref, vst_ref, kw_ref, vwt_ref,
                      kb_ref, vbt_ref, sza_ref, szb_ref, sma_ref, smb_ref, gt_ref, kv_scr.at[half])


def _project_rows(r0, hm, x_ref, shift_ref, scale_ref, gpre_ref, w_ref,
                  qa_ref, qb_ref, t16_ref, ks_ref, vst_ref, kw_ref, vwt_ref,
                  kb_ref, vbt_ref, sza_ref, szb_ref, sma_ref, smb_ref, gt_ref, kv_scr):
    rows = pl.ds(r0, hm)
    xf = x_ref[0, rows, :]
    ms = jnp.mean(xf * xf, axis=-1, keepdims=True)
    y = xf * lax.rsqrt(ms + RMS_EPS) * gpre_ref[...]
    h = y * (1.0 + scale_ref[0]) + shift_ref[0]
    hb = h.astype(jnp.bfloat16)

    lane = _iota((hm, FEAT), 1)
    pos = pl.program_id(1) * x_ref.shape[1] + r0 + _iota((hm, FEAT), 0)
    key_feat = _key_features(lane, pos)

    def store_queries(acc, q_ref):
        for head in range(N_HEADS):
            q_feat = _query_features(lane, 2.0 ** -(head + 1))
            g, r, odd = head // GROUP, head % GROUP, head % 2
            c = (head // 2) * FEAT
            q_ref[0, g, r, rows, :] = _with_features(acc[:, c:c + FEAT], odd, q_feat, lane)

    def store_keys(pair, k_ref):
        for g in range(N_KV):
            k_ref[0, g, rows, :] = _with_features(pair, g, key_feat, lane)

    def store_values(pair, vt_ref):
        vt = pair.T
        ones_rows = jnp.where(_iota((V_ROWS - HEAD_DIM, hm), 0) == 0, 1.0, 0.0).astype(jnp.bfloat16)
        for g in range(N_KV):
            head = vt[g * HEAD_DIM:(g + 1) * HEAD_DIM].astype(jnp.bfloat16)
            vt_ref[0, g, :, rows] = jnp.concatenate([head, ones_rows], axis=0)

    acc = _dot(hb, w_ref[:, _C_QA:_C_G])
    store_queries(acc, qa_ref)
    acc = acc[:, _C_KVA:]
    for a in range(2):
        kv_scr[a] = acc[:, a * KVW:(a + 1) * KVW]
        kv_scr[2 + a] = pltpu.roll(acc[:, a * KVW:(a + 1) * KVW], HEAD_DIM, 1)
    groups = pl.ds(r0 // CMP_STRIDE, hm // CMP_STRIDE)
    for l in range(CMP_STRIDE):
        for a in range(2):
            plain = kv_scr[a, pl.ds(l, hm // CMP_STRIDE, stride=CMP_STRIDE), :]
            rolled = kv_scr[2 + a, pl.ds(l, hm // CMP_STRIDE, stride=CMP_STRIDE), :]
            side = (l % 2) * HEAD_DIM
            for g in range(N_KV):
                tok = (plain if g % 2 == l % 2 else rolled)[:, side:side + HEAD_DIM]
                t16_ref[a, 0, g, groups, l * HEAD_DIM:(l + 1) * HEAD_DIM] = tok.astype(jnp.bfloat16)
    store_keys(acc[:, 2 * KVW:3 * KVW], ks_ref)
    store_values(acc[:, 3 * KVW:4 * KVW], vst_ref)
    store_keys(acc[:, 4 * KVW:5 * KVW], kw_ref)
    store_values(acc[:, 5 * KVW:6 * KVW], vwt_ref)

    acc = _dot(hb, w_ref[:, _C_G:_C_ZA])
    gt_ref[0, :, rows] = _sigmoid(acc).T[:GATE_ROWS]

    acc = _dot(hb, w_ref[:, _C_ZA:_C_M])
    za = acc[:, :WIDTH]
    sza_ref[0, rows, :] = (za * _sigmoid(za)).astype(jnp.bfloat16)
    store_queries(acc[:, _C_QB - _C_ZA:_C_KVB - _C_ZA], qb_ref)
    store_keys(acc[:, _C_KVB - _C_ZA:_C_KVB - _C_ZA + KVW], kb_ref)
    store_values(acc[:, _C_KVB - _C_ZA + KVW:_C_ZB - _C_ZA], vbt_ref)
    zb = acc[:, _C_ZB - _C_ZA:]
    szb_ref[0, rows, :] = (zb * _sigmoid(zb)).astype(jnp.bfloat16)

    acc = _dot(hb, w_ref[:, _C_M:_C_M + D_MODEL])
    sma_ref[0, rows, :] = _sigmoid(acc).astype(jnp.bfloat16)
    acc = _dot(hb, w_ref[:, _C_M + D_MODEL:_C_END])
    smb_ref[0, rows, :] = _sigmoid(acc).astype(jnp.bfloat16)


def _padded_init(B, S):
    pad_row = jnp.zeros((FEAT,), jnp.float32).at[LANE_BLK:LANE_POS].set(PAD_KEY_BLOCK)
    k_init = jnp.broadcast_to(pad_row.astype(jnp.bfloat16), (B, N_KV, KEY_PAD + S, FEAT))
    vt_init = jnp.zeros((B, N_KV, V_ROWS, KEY_PAD + S), jnp.bfloat16)
    return k_init, vt_init


def _in_proj(x, shift, scale, gpre, w_in_t, layer):
    B, S, D = x.shape
    assert w_in_t.shape[1] + GATE_PAD - N_GATES == _C_END
    tm = TM_PROJ
    assert KEY_PAD == tm and KEY_PAD >= max(NSA_WINDOW, SWA_WINDOW)
    bf = jnp.bfloat16
    q_shape = jax.ShapeDtypeStruct((B, N_KV, GROUP, S, FEAT), bf)
    t16_shape = jax.ShapeDtypeStruct((2, B, N_KV, S // CMP_STRIDE, CMP_STRIDE * HEAD_DIM), bf)
    k_shape = jax.ShapeDtypeStruct((B, N_KV, S, FEAT), bf)
    vt_shape = jax.ShapeDtypeStruct((B, N_KV, V_ROWS, S), bf)
    q_spec = pl.BlockSpec((1, N_KV, GROUP, tm, FEAT), lambda b, i: (b, 0, 0, i, 0))
    t16_spec = pl.BlockSpec((2, 1, N_KV, tm // CMP_STRIDE, CMP_STRIDE * HEAD_DIM),
                            lambda b, i: (0, b, 0, i, 0))
    k_spec = pl.BlockSpec((1, N_KV, tm, FEAT), lambda b, i: (b, 0, i, 0))
    vt_spec = pl.BlockSpec((1, N_KV, V_ROWS, tm), lambda b, i: (b, 0, 0, i))
    kp_shape = jax.ShapeDtypeStruct((B, N_KV, KEY_PAD + S, FEAT), bf)
    vtp_shape = jax.ShapeDtypeStruct((B, N_KV, V_ROWS, KEY_PAD + S), bf)
    kp_spec = pl.BlockSpec((1, N_KV, tm, FEAT), lambda b, i: (b, 0, i + 1, 0))
    vtp_spec = pl.BlockSpec((1, N_KV, V_ROWS, tm), lambda b, i: (b, 0, 0, i + 1))
    k_init, vt_init = _padded_init(B, S)
    any_spec = pl.BlockSpec(memory_space=pl.ANY)
    row_spec = lambda n: pl.BlockSpec((1, tm, n), lambda b, i: (b, i, 0))
    vec_spec = pl.BlockSpec((1, 1, D), lambda b, i: (b, 0, 0))
    return pl.pallas_call(
        functools.partial(_in_proj_kernel, layer),
        out_shape=(q_shape, q_shape, t16_shape, k_shape, vt_shape, kp_shape, vtp_shape,
                   kp_shape, vtp_shape,
                   jax.ShapeDtypeStruct((B, S, WIDTH), bf), jax.ShapeDtypeStruct((B, S, WIDTH), bf),
                   jax.ShapeDtypeStruct((B, S, D), bf), jax.ShapeDtypeStruct((B, S, D), bf),
                   jax.ShapeDtypeStruct((B, GATE_ROWS, S), jnp.float32)),
        grid=(B, S // tm),
        in_specs=[row_spec(D), vec_spec, vec_spec,
                  pl.BlockSpec((1, D), lambda b, i: (0, 0)),
                  any_spec, any_spec, any_spec, any_spec, any_spec],
        input_output_aliases={5: 5, 6: 6, 7: 7, 8: 8},
        out_specs=(q_spec, q_spec, t16_spec, k_spec, vt_spec, kp_spec, vtp_spec,
                   kp_spec, vtp_spec,
                   row_spec(WIDTH), row_spec(WIDTH), row_spec(D), row_spec(D),
                   pl.BlockSpec((1, GATE_ROWS, tm), lambda b, i: (b, 0, i))),
        scratch_shapes=[pltpu.VMEM((2, 4, tm // 2, KVW), jnp.float32),
                        pltpu.VMEM((D, _C_END), bf),
                        pltpu.VMEM((2, W_CHUNK, D), jnp.float32),
                        pltpu.SemaphoreType.DMA((2,))],
        compiler_params=pltpu.CompilerParams(
            dimension_semantics=("arbitrary", "arbitrary"), vmem_limit_bytes=VMEM_LIMIT),
        name="in_proj",
    )(x, shift, scale, gpre, w_in_t, k_init, vt_init, k_init, vt_init)


def _compress_kernel(t_ref, pe_ref, w1_ref, w2_ref, o_ref, ot_ref, w1b_scr, bias_scr):
    half = CMP_STRIDE * HEAD_DIM

    @pl.when(pl.program_id(1) == 0)
    def _():
        w1 = w1_ref[0]
        w1b_scr[...] = w1.astype(jnp.bfloat16)
        pe = jnp.broadcast_to(pe_ref[0], (bias_scr.shape[0], 2 * half))
        bias_scr[...] = jnp.dot(pe, w1, preferred_element_type=jnp.float32,
                                precision=lax.Precision.HIGHEST)

    t = t_ref[0, 0]
    first = _dot(t, w1b_scr[:half])
    second = _dot(t, w1b_scr[half:])
    n = second.shape[0]
    second = pltpu.roll(second, n - 1, 0)
    pre = first + second + bias_scr[0:1]
    hid = pre * _sigmoid(pre)
    out = _dot(hid.astype(jnp.bfloat16), w2_ref[0].astype(jnp.bfloat16))
    lane = _iota((n, FEAT), 1)
    c = _iota((n, FEAT), 0)
    end = c * CMP_STRIDE + (CMP_BLOCK - 1)
    o_ref[0, 0] = jnp.where(lane < HEAD_DIM, out, _key_features(lane, end)).astype(jnp.bfloat16)
    ot_ref[0, 0] = out.T[:HEAD_DIM].astype(jnp.bfloat16)


def _compress(t, pe, w1, w2):
    _, BG, n, wdt = t.shape
    return pl.pallas_call(
        _compress_kernel,
        out_shape=(jax.ShapeDtypeStruct((2, BG, n, FEAT), jnp.bfloat16),
                   jax.ShapeDtypeStruct((2, BG, HEAD_DIM, n), jnp.bfloat16)),
        grid=(2, BG),
        in_specs=[pl.BlockSpec((1, 1, n, wdt), lambda a, i: (a, i, 0, 0)),
                  pl.BlockSpec((1, 1, 2 * wdt), lambda a, i: (a, 0, 0)),
                  pl.BlockSpec((1, 2 * wdt, CMP_HIDDEN), lambda a, i: (a, 0, 0)),
                  pl.BlockSpec((1, CMP_HIDDEN, FEAT), lambda a, i: (a, 0, 0))],
        out_specs=(pl.BlockSpec((1, 1, n, FEAT), lambda a, i: (a, i, 0, 0)),
                   pl.BlockSpec((1, 1, HEAD_DIM, n), lambda a, i: (a, i, 0, 0))),
        scratch_shapes=[pltpu.VMEM((2 * wdt, CMP_HIDDEN), jnp.bfloat16),
                        pltpu.VMEM((8, CMP_HIDDEN), jnp.float32)],
        compiler_params=pltpu.CompilerParams(dimension_semantics=("arbitrary", "arbitrary")),
        name="compress",
    )(t, pe, w1, w2)


def _head_slope(g, r):
    return jnp.where(g == 0, 1.0, 2.0 ** -GROUP).astype(jnp.float32) * (2.0 ** -(r + 1))


def _chunked_max(rows, n_rows):
    m8 = None
    for c in range(0, n_rows, CHUNK):
        chunk = rows(c)
        for r in range(0, CHUNK, 8):
            m8 = chunk[r:r + 8] if m8 is None else jnp.maximum(m8, chunk[r:r + 8])
    return jnp.max(m8, axis=0, keepdims=True)


def _chunked_exp2(rows, n_rows, m):
    return jnp.concatenate(
        [jnp.exp2(rows(c) - m).astype(jnp.bfloat16) for c in range(0, n_rows, CHUNK)], axis=0)


def _per_head(x):
    return jnp.concatenate([x] * GROUP, axis=1)


def _band_scores(q, k_ref, q0, window):
    start = pl.multiple_of(q0 + (KEY_PAD - window), BAND_ALIGN)
    return _dot_nt(k_ref[0, 0, pl.ds(start, window + TQ), :], q)


def _band_finish(st, vt_ref, q0, window, sink_row):
    n_keys = window + TQ
    start = pl.multiple_of(q0 + (KEY_PAD - window), BAND_ALIGN)
    col = _iota((CHUNK, TQ), 1)

    def rows(c):
        part = st[c:c + CHUNK]
        row = c + _iota((CHUNK, TQ), 0)
        if c < TQ:
            part = part + _per_head(jnp.where(row > col, 0.0, NEG))
        if c + CHUNK - 1 > window:
            part = part + _per_head(jnp.where(row <= col + window, 0.0, NEG))
        return part

    m = _chunked_max(rows, n_keys)
    if sink_row is not None:
        m = jnp.maximum(m, sink_row)
    p = _chunked_exp2(rows, n_keys, m)
    o = _dot(vt_ref[0, 0, :, pl.ds(start, n_keys)], p)
    l = o[HEAD_DIM:HEAD_DIM + 1]
    if sink_row is not None:
        l = l + jnp.exp2(sink_row - m)
    return o[:HEAD_DIM] / l


def _store_heads(o_t, sz_ref, o_ref):
    stacked = jnp.concatenate([o_t[:, r * TQ:(r + 1) * TQ] for r in range(GROUP)], axis=0)
    o_ref[0] = (stacked.T * sz_ref[0].astype(jnp.float32)).astype(jnp.bfloat16)


def _importance_matrix(nsb, n_cmp):
    ratio = SLC_BLOCK // CMP_STRIDE
    span = CMP_BLOCK // CMP_STRIDE
    off = ratio * np.arange(nsb)[:, None] - np.arange(n_cmp)[None, :]
    w = sum((off == mm + nn).astype(np.float32) for mm in range(ratio) for nn in range(span))
    w[:, n_cmp - 1] = 0.0
    return jnp.asarray(w, jnp.bfloat16)


def _selection_rank(score):
    nsb = score.shape[0]
    groups = [score[8 * a:8 * a + 8] for a in range(nsb // 8)]
    sub = _iota((8, TQ), 0)
    rank = [jnp.zeros((8, TQ), jnp.float32) for _ in groups]
    for i in range(nsb):
        row = jnp.broadcast_to(score[i:i + 1, :], (8, TQ))
        for a, grp in enumerate(groups):
            if a < i // 8:
                ahead = jnp.where(row > grp, 1.0, 0.0)
            elif a > i // 8:
                ahead = jnp.where(row >= grp, 1.0, 0.0)
            else:
                ahead = jnp.where(sub > i % 8, jnp.where(row >= grp, 1.0, 0.0),
                                  jnp.where(row > grp, 1.0, 0.0))
            rank[a] = rank[a] + ahead
    return jnp.concatenate(rank, axis=0)


def _mixers_kernel(sinks_ref, qa_ref, qb_ref, kc_ref, vct_ref, ks_ref, vst_ref, kw_ref, vwt_ref,
                   kb_ref, vbt_ref, onehot_ref, wimp_ref, gt_ref, sza_ref, szb_ref, oa_ref, ob_ref,
                   s_buf):
    g = pl.program_id(1)
    qi = pl.program_id(2)
    q0 = qi * TQ
    q = qa_ref[0, 0].reshape(GROUP * TQ, FEAT)

    n_cmp = kc_ref.shape[1]
    st_cmp = _dot_nt(kc_ref[0], q)
    st_swa = _band_scores(qb_ref[0, 0].reshape(GROUP * TQ, FEAT), kb_ref, q0, SWA_WINDOW)
    st_win = _band_scores(q, kw_ref, q0, NSA_WINDOW)

    qpos_c = q0 + _iota((n_cmp, TQ), 1)
    end_c = _iota((n_cmp, TQ), 0) * CMP_STRIDE + (CMP_BLOCK - 1)
    hidden = jnp.where(qpos_c >= end_c, 0.0, NEG)
    any_valid = (qpos_c[0:1] >= CMP_BLOCK - 1).astype(jnp.float32)
    p_sum = jnp.zeros((n_cmp, TQ), jnp.float32)
    o_cmp = []
    for r in range(GROUP):
        sr = st_cmp[:, r * TQ:(r + 1) * TQ] + hidden
        er = jnp.exp2(sr - jnp.max(sr, axis=0, keepdims=True))
        inv = any_valid / jnp.sum(er, axis=0, keepdims=True)
        p_sum = p_sum + er * inv
        o_cmp.append(_dot(vct_ref[0], er.astype(jnp.bfloat16)) * inv)
    o_cmp = jnp.concatenate(o_cmp, axis=1)

    wimp = wimp_ref[...]
    p_hi = p_sum.astype(jnp.bfloat16)
    rem = p_sum - p_hi.astype(jnp.float32)
    p_mid = rem.astype(jnp.bfloat16)
    p_lo = (rem - p_mid.astype(jnp.float32)).astype(jnp.bfloat16)
    imp = _dot(wimp, p_hi) + _dot(wimp, p_mid) + _dot(wimp, p_lo)

    qpos = (q0 + _iota((1, TQ), 1)).astype(jnp.float32)
    qpos_log2 = sum(qpos * piece for piece in LOG2E_PIECES)
    sink_row = jnp.concatenate(
        [sinks_ref[g * GROUP + r] * LOG2E + _head_slope(g, r) * qpos_log2 for r in range(GROUP)],
        axis=1)
    _store_heads(_band_finish(st_swa, vbt_ref, q0, SWA_WINDOW, sink_row), szb_ref, ob_ref)

    o_win = _band_finish(st_win, vwt_ref, q0, NSA_WINDOW, None)

    nsb = imp.shape[0]
    jrow = _iota((nsb, TQ), 0)
    cur = lax.shift_right_logical(q0 + _iota((nsb, TQ), 1), 6)
    causal = jrow <= cur
    forced = (jrow == 0) | (jrow == cur) | (jrow == cur - 1)
    score = jnp.where(causal, jnp.where(forced, BIG, imp), NEG)
    rank = _selection_rank(score)
    unselected = jnp.where(causal, jnp.where(rank < float(N_SELECT), 0.0, NEG_FEATURE), NEG_FEATURE)
    mask_feat = jnp.concatenate([unselected, jnp.zeros((FEAT - nsb, TQ), jnp.float32)], axis=0).T
    q_sel = jnp.concatenate(
        [q, jnp.concatenate([mask_feat.astype(jnp.bfloat16)] * GROUP, axis=0)], axis=1)

    tk = TK_SLC
    diag_pair = (q0 + TQ - 1) // (2 * tk)

    def slc_scores(j):
        k0 = pl.multiple_of(j * tk, tk)
        k_aug = jnp.concatenate([ks_ref[0, 0, pl.ds(k0, tk), :], onehot_ref[pl.ds(k0, tk), :]], axis=1)
        return _dot_nt(k_aug, q_sel)

    def slc_update(j, slot, carry, near_diagonal):
        m, acc = carry
        k0 = pl.multiple_of(j * tk, tk)

        def rows(c):
            s = s_buf[slot, c:c + CHUNK, :]
            if near_diagonal:
                dist = (q0 + _iota((CHUNK, TQ), 1)) - (k0 + c + _iota((CHUNK, TQ), 0))
                s = s + _per_head(jnp.where(dist >= 0, 0.0, NEG))
            return s

        m_new = jnp.maximum(m, _chunked_max(rows, tk))
        alpha = jnp.exp2(m - m_new)
        p = _chunked_exp2(rows, tk, m_new)
        acc = alpha * acc + _dot(vst_ref[0, 0, :, pl.ds(k0, tk)], p)
        return m_new, acc

    s_buf[2] = slc_scores(2 * diag_pair)
    s_buf[3] = slc_scores(2 * diag_pair + 1)
    s_buf[0] = slc_scores(0)

    n = GROUP * TQ
    carry = (jnp.full((1, n), M_FLOOR, jnp.float32), jnp.zeros((V_ROWS, n), jnp.float32))
    carry = slc_update(2 * diag_pair, 2, carry, True)
    carry = slc_update(2 * diag_pair + 1, 3, carry, True)

    def slc_pair(jj, carry):
        j = 2 * jj
        s_buf[1] = slc_scores(j + 1)
        carry = slc_update(j, 0, carry, False)
        s_buf[0] = slc_scores(j + 2)
        return slc_update(j + 1, 1, carry, False)

    _, acc = lax.fori_loop(0, diag_pair, slc_pair, carry)
    o_slc = acc[:HEAD_DIM] / acc[HEAD_DIM:HEAD_DIM + 1]

    pieces = []
    for r in range(GROUP):
        sl = slice(r * TQ, (r + 1) * TQ)
        gate = [gt_ref[0, pl.ds(br * N_HEADS + g * GROUP + r, 1), :] for br in range(3)]
        pieces.append(gate[0] * o_cmp[:, sl] + gate[1] * o_slc[:, sl] + gate[2] * o_win[:, sl])
    _store_heads(jnp.concatenate(pieces, axis=1), sza_ref, oa_ref)


def _mixers(sinks, qa, qb, kc, vct, ks, vst, kw, vwt, kb, vbt, onehot, wimp, gt, sza, szb):
    B, _, _, S, _ = qa.shape
    n_cmp = kc.shape[1]
    q_spec = pl.BlockSpec((1, 1, GROUP, TQ, FEAT), lambda b, g, i: (b, g, 0, i, 0))
    seq_k = lambda n: pl.BlockSpec((1, 1, n, FEAT), lambda b, g, i: (b, g, 0, 0))
    seq_vt = lambda n: pl.BlockSpec((1, 1, V_ROWS, n), lambda b, g, i: (b, g, 0, 0))
    o_spec = pl.BlockSpec((1, TQ, GROUP * HEAD_DIM), lambda b, g, i: (b, i, g))
    o_shape = jax.ShapeDtypeStruct((B, S, WIDTH), jnp.bfloat16)
    return pl.pallas_call(
        _mixers_kernel,
        out_shape=(o_shape, o_shape),
        grid=(B, N_KV, S // TQ),
        in_specs=[pl.BlockSpec(memory_space=pltpu.SMEM), q_spec, q_spec,
                  pl.BlockSpec((1, n_cmp, FEAT), lambda b, g, i: (b * N_KV + g, 0, 0)),
                  pl.BlockSpec((1, HEAD_DIM, n_cmp), lambda b, g, i: (b * N_KV + g, 0, 0)),
                  seq_k(S), seq_vt(S), seq_k(kw.shape[2]), seq_vt(kw.shape[2]),
                  seq_k(kb.shape[2]), seq_vt(kb.shape[2]),
                  pl.BlockSpec((S, FEAT), lambda b, g, i: (0, 0)),
                  pl.BlockSpec(wimp.shape, lambda b, g, i: (0, 0)),
                  pl.BlockSpec((1, GATE_ROWS, TQ), lambda b, g, i: (b, 0, i)),
                  o_spec, o_spec],
        out_specs=(o_spec, o_spec),
        scratch_shapes=[pltpu.VMEM((4, TK_SLC, GROUP * TQ), jnp.float32)],
        compiler_params=pltpu.CompilerParams(
            dimension_semantics=("parallel", "parallel", "arbitrary"),
            vmem_limit_bytes=VMEM_LIMIT),
        name="mixers",
    )(sinks, qa, qb, kc, vct, ks, vst, kw, vwt, kb, vbt, onehot, wimp, gt, sza, szb)


def _out_proj_kernel(x_ref, oza_ref, ozb_ref, sma_ref, smb_ref, gate_ref, gpost_ref,
                     woa_ref, wob_ref, wout_ref, o_ref, woa_b, wob_b, wout_b):
    @pl.when((pl.program_id(0) == 0) & (pl.program_id(1) == 0))
    def _():
        woa_b[...] = woa_ref[...].astype(jnp.bfloat16)
        wob_b[...] = wob_ref[...].astype(jnp.bfloat16)
        wout_b[...] = wout_ref[...].astype(jnp.bfloat16)

    tm = x_ref.shape[1]
    for rows in (pl.ds(0, tm // 2), pl.ds(tm // 2, tm // 2)):
        ya = _dot(oza_ref[0, rows, :], woa_b[...])
        yb = _dot(ozb_ref[0, rows, :], wob_b[...])
        y = (sma_ref[0, rows, :].astype(jnp.float32) * ya
             + smb_ref[0, rows, :].astype(jnp.float32) * yb)
        yo = _dot(y.astype(jnp.bfloat16), wout_b[...])
        ms = jnp.mean(yo * yo, axis=-1, keepdims=True)
        normed = yo * lax.rsqrt(ms + RMS_EPS) * gpost_ref[...]
        o_ref[0, rows, :] = x_ref[0, rows, :] + gate_ref[0] * normed


def _out_proj(x, oza, ozb, sma, smb, gate, gpost, woa, wob, wout):
    B, S, D = x.shape
    tm = TM_PROJ
    row_spec = lambda n: pl.BlockSpec((1, tm, n), lambda b, i: (b, i, 0))
    const = lambda shape: pl.BlockSpec(shape, lambda b, i: (0,) * len(shape))
    weight = lambda shape: pl.BlockSpec(shape, lambda b, i: (0, 0), pipeline_mode=pl.Buffered(1))
    return pl.pallas_call(
        _out_proj_kernel,
        out_shape=jax.ShapeDtypeStruct((B, S, D), jnp.float32),
        grid=(B, S // tm),
        in_specs=[row_spec(D), row_spec(WIDTH), row_spec(WIDTH), row_spec(D), row_spec(D),
                  pl.BlockSpec((1, 1, D), lambda b, i: (b, 0, 0)), const((1, D)),
                  weight((WIDTH, D)), weight((WIDTH, D)), weight((D, D))],
        out_specs=row_spec(D),
        scratch_shapes=[pltpu.VMEM((WIDTH, D), jnp.bfloat16), pltpu.VMEM((WIDTH, D), jnp.bfloat16),
                        pltpu.VMEM((D, D), jnp.bfloat16)],
        compiler_params=pltpu.CompilerParams(
            dimension_semantics=("arbitrary", "arbitrary"), vmem_limit_bytes=VMEM_LIMIT),
        name="out_proj",
    )(x, oza, ozb, sma, smb, gate, gpost, woa, wob, wout)


def _block_onehot(S):
    blk = jnp.arange(S, dtype=jnp.int32)[:, None] // SLC_BLOCK
    return (blk == jnp.arange(FEAT, dtype=jnp.int32)[None, :]).astype(jnp.bfloat16)


def kernel(x, c, w_ada, b_ada, g_pre, g_post, w_in, pe_cmp_k, pe_cmp_v, w_cmp_k1, w_cmp_k2,
           w_cmp_v1, w_cmp_v2, w_o_nsa, w_o_swa, w_out, sinks):
    B, S, D = x.shape
    depth = w_in.shape[0]
    for l in range(depth):
        c8 = jnp.pad(c, ((0, 8 - B), (0, 0)))
        mod = _adaln_mod(c8, w_ada[l], b_ada[l][None, :])[:B]
        shift, scale, gate = (mod[:, None, i * D:(i + 1) * D] for i in range(3))

        (qa, qb, t16, ks, vst, kw, vwt, kb, vbt, sza, szb, sma, smb, gt) = _in_proj(
            x, shift, scale, g_pre[l][None, :], jnp.swapaxes(w_in, 1, 2), l)

        rows = S // CMP_STRIDE
        t = t16.reshape(2, B * N_KV, rows, CMP_STRIDE * HEAD_DIM)
        pe = jnp.stack([pe_cmp_k[l], pe_cmp_v[l]]).reshape(2, 1, CMP_BLOCK * HEAD_DIM)
        w2 = jnp.pad(jnp.stack([w_cmp_k2[l], w_cmp_v2[l]]), ((0, 0), (0, 0), (0, FEAT - HEAD_DIM)))
        cmp_o, cmp_ot = _compress(t, pe, jnp.stack([w_cmp_k1[l], w_cmp_v1[l]]), w2)

        oza, ozb = _mixers(sinks[l], qa, qb, cmp_o[0], cmp_ot[1], ks, vst, kw, vwt, kb, vbt,
                           _block_onehot(S), _importance_matrix(S // SLC_BLOCK, rows), gt, sza, szb)

        x = _out_proj(x, oza, ozb, sma, smb, gate, g_post[l][None, :],
                      w_o_nsa[l], w_o_swa[l], w_out[l])
    return x
```
